```python
import math
import jax, jax.numpy as jnp
from jax import lax
import numpy as np

D_MODEL = 1024
BATCH = 8
SEQ = 2048
DEPTH = 4

N_MIXERS = 2
N_GLA = (DEPTH + 1) // 2
N_MLA = DEPTH // 2
EPS = 1e-6

D_FF = 2816

GLA_HEADS = 4
GLA_DK_TOT = D_MODEL // 2
GLA_DV_TOT = D_MODEL
GLA_DK = GLA_DK_TOT // GLA_HEADS
GLA_DV = GLA_DV_TOT // GLA_HEADS
GLA_GATE_RANK = 16
GLA_TAU = 16.0
GLA_CHUNK = 64
GLA_IN = 2 * GLA_DK_TOT + 2 * GLA_DV_TOT + 2 * GLA_GATE_RANK

MLA_HEADS = 8
MLA_NOPE = 128
MLA_ROPE = 64
MLA_V = 128
MLA_Q_RANK = 768
MLA_KV_RANK = 256
MLA_QK = MLA_NOPE + MLA_ROPE
MLA_IN = MLA_Q_RANK + MLA_KV_RANK + MLA_ROPE
ROPE_THETA = 10000.0
Q_BLOCK = 128
MAX_POS_OFFSET = 4096

kernel_name = "hybrid_gla_mla_macaron_encoder"


def rms_norm(x, g):
    xf = x.astype(jnp.float32)
    y = xf * lax.rsqrt(jnp.mean(xf * xf, axis=-1, keepdims=True) + EPS)
    return (y * g.astype(jnp.float32)).astype(x.dtype)


def swiglu(h, w_gu, w_down):
    gate, up = jnp.split(h @ w_gu, [D_FF], axis=-1)
    return (jax.nn.silu(gate) * up) @ w_down


def gla_direction(q, k, v, log_a, include_diag):
    bsz, nh, seq, dk = q.shape
    dv = v.shape[-1]
    n = seq // GLA_CHUNK

    def to_chunks(t):
        return jnp.moveaxis(t.reshape(bsz, nh, n, GLA_CHUNK, t.shape[-1]), 2, 0)

    qc, kc, vc = to_chunks(q), to_chunks(k), to_chunks(v)
    bc = jnp.cumsum(to_chunks(log_a), axis=-2)
    mask = jnp.tril(jnp.ones((GLA_CHUNK, GLA_CHUNK), dtype=bool), k=0 if include_diag else -1)

    def step(state, inp):
        q_, k_, v_, b_ = inp
        diff = b_[:, :, :, None, :] - b_[:, :, None, :, :]
        decay = jnp.exp(jnp.where(mask[:, :, None], diff, -jnp.inf))
        scores = jnp.einsum('bhtd,bhsd,bhtsd->bhts', q_, k_, decay)
        out = (jnp.einsum('bhts,bhsv->bhtv', scores, v_)
               + jnp.einsum('bhtd,bhdv->bhtv', q_ * jnp.exp(b_), state))
        b_end = b_[:, :, -1:, :]
        state = (state * jnp.exp(b_end)[:, :, 0, :, None]
                 + jnp.einsum('bhsd,bhsv->bhdv', k_ * jnp.exp(b_end - b_), v_))
        return state, out

    state0 = jnp.zeros((bsz, nh, dk, dv), jnp.float32)
    _, out = lax.scan(step, state0, (qc, kc, vc, bc))
    return jnp.moveaxis(out, 0, 2).reshape(bsz, nh, seq, dv)


def gla_mixer(h, w_in, w_gate2, b_gate, head_norm, w_out):
    bsz, seq, _ = h.shape
    splits = [GLA_DK_TOT, 2 * GLA_DK_TOT, 2 * GLA_DK_TOT + GLA_DV_TOT,
              2 * GLA_DK_TOT + 2 * GLA_DV_TOT, 2 * GLA_DK_TOT + 2 * GLA_DV_TOT + GLA_GATE_RANK]
    q, k, v, r, g_fw, g_bw = jnp.split(h @ w_in, splits, axis=-1)

    def heads(t, d):
        return t.reshape(bsz, seq, GLA_HEADS, d).transpose(0, 2, 1, 3).astype(jnp.float32)

    q = heads(q, GLA_DK) * (GLA_DK ** -0.5)
    k = heads(k, GLA_DK)
    v = heads(v, GLA_DV)
    log_fw = heads(jax.nn.log_sigmoid(g_fw @ w_gate2[0] + b_gate[0]), GLA_DK) / GLA_TAU
    log_bw = heads(jax.nn.log_sigmoid(g_bw @ w_gate2[1] + b_gate[1]), GLA_DK) / GLA_TAU

    flip = lambda t: t[:, :, ::-1, :]
    o_fw = gla_direction(q, k, v, log_fw, True)
    o_bw = flip(gla_direction(flip(q), flip(k), flip(v), flip(log_bw), False))
    o = (o_fw + o_bw).transpose(0, 2, 1, 3)
    o = rms_norm(o, head_norm).astype(h.dtype)
    o = o * jax.nn.silu(r).reshape(bsz, seq, GLA_HEADS, GLA_DV)
    return o.reshape(bsz, seq, GLA_DV_TOT) @ w_out


def apply_rope(x, cos, sin):
    c = cos[:, :, None, :].astype(x.dtype)
    s = sin[:, :, None, :].astype(x.dtype)
    x1, x2 = jnp.split(x, 2, axis=-1)
    return jnp.concatenate([x1 * c - x2 * s, x2 * c + x1 * s], axis=-1)


def mla_mixer(h, cos, sin, w_in, q_norm, kv_norm, w_uq, w_ukv, w_out):
    bsz, seq, _ = h.shape
    c_q, c_kv, k_pe = jnp.split(h @ w_in, [MLA_Q_RANK, MLA_Q_RANK + MLA_KV_RANK], axis=-1)
    q = (rms_norm(c_q, q_norm) @ w_uq).reshape(bsz, seq, MLA_HEADS, MLA_QK)
    q_nope, q_pe = jnp.split(q, [MLA_NOPE], axis=-1)
    kv = (rms_norm(c_kv, kv_norm) @ w_ukv).reshape(bsz, seq, MLA_HEADS, MLA_NOPE + MLA_V)
    k_nope, v = jnp.split(kv, [MLA_NOPE], axis=-1)
    q_pe = apply_rope(q_pe, cos, sin)
    k_pe = apply_rope(k_pe[:, :, None, :], cos, sin)
    q_full = jnp.concatenate([q_nope, q_pe], axis=-1).transpose(0, 2, 1, 3)
    k_full = jnp.concatenate(
        [k_nope, jnp.broadcast_to(k_pe, (bsz, seq, MLA_HEADS, MLA_ROPE))], axis=-1
    ).transpose(0, 2, 1, 3)
    v = v.transpose(0, 2, 1, 3)
    scale = MLA_QK ** -0.5
    n_blk = seq // Q_BLOCK
    q_blocks = jnp.moveaxis(q_full.reshape(bsz, MLA_HEADS, n_blk, Q_BLOCK, MLA_QK), 2, 0)

    def attend(qb):
        s = jnp.einsum('bhqd,bhkd->bhqk', qb, k_full).astype(jnp.float32) * scale
        p = jax.nn.softmax(s, axis=-1).astype(v.dtype)
        return jnp.einsum('bhqk,bhkv->bhqv', p, v)

    out = lax.map(attend, q_blocks)
    out = jnp.moveaxis(out, 0, 2).reshape(bsz, MLA_HEADS, seq, MLA_V)
    out = out.transpose(0, 2, 1, 3).reshape(bsz, seq, MLA_HEADS * MLA_V)
    return out @ w_out


def setup_inputs(seed: int = 0) -> dict:
    key = jax.random.key(seed)
    ks = jax.random.split(key, 24)

    def dense(k, shape, fan_in):
        return jax.random.normal(k, shape, jnp.float32) * (fan_in ** -0.5)

    def gain(k, shape):
        return 1.0 + 0.02 * jax.random.normal(k, shape, jnp.float32)

    x = jax.random.normal(ks[0], (BATCH, SEQ, D_MODEL), jnp.float32)
    positions = (jnp.arange(SEQ, dtype=jnp.int32)[None, :]
                 + jax.random.randint(ks[1], (BATCH, 1), 0, MAX_POS_OFFSET, dtype=jnp.int32))
    return {
        "x": x,
        "positions": positions,
        "ffn_norm": gain(ks[2], (DEPTH, 2, D_MODEL)),
        "ffn_w_gu": dense(ks[3], (DEPTH, 2, D_MODEL, 2 * D_FF), D_MODEL),
        "ffn_w_down": dense(ks[4], (DEPTH, 2, D_FF, D_MODEL), D_FF),
        "mix_norm": gain(ks[5], (DEPTH, D_MODEL)),
        "gla_w_in": dense(ks[6], (N_GLA, D_MODEL, GLA_IN), D_MODEL),
        "gla_w_gate2": dense(ks[7], (N_GLA, 2, GLA_GATE_RANK, GLA_DK_TOT), GLA_GATE_RANK),
        "gla_b_gate": 0.1 * jax.random.normal(ks[8], (N_GLA, 2, GLA_DK_TOT), jnp.float32),
        "gla_head_norm": gain(ks[9], (N_GLA, GLA_DV)),
        "gla_w_out": dense(ks[10], (N_GLA, GLA_DV_TOT, D_MODEL), GLA_DV_TOT),
        "mla_w_in": dense(ks[11], (N_MLA, D_MODEL, MLA_IN), D_MODEL),
        "mla_q_norm": gain(ks[12], (N_MLA, MLA_Q_RANK)),
        "mla_kv_norm": gain(ks[13], (N_MLA, MLA_KV_RANK)),
        "mla_w_uq": dense(ks[14], (N_MLA, MLA_Q_RANK, MLA_HEADS * MLA_QK), MLA_Q_RANK),
        "mla_w_ukv": dense(ks[15], (N_MLA, MLA_KV_RANK, MLA_HEADS * (MLA_NOPE + MLA_V)), MLA_KV_RANK),
        "mla_w_out": dense(ks[16], (N_MLA, MLA_HEADS * MLA_V, D_MODEL), MLA_HEADS * MLA_V),
        "final_norm": gain(ks[17], (D_MODEL,)),
    }


def reference(x, positions, ffn_norm, ffn_w_gu, ffn_w_down, mix_norm,
              gla_w_in, gla_w_gate2, gla_b_gate, gla_head_norm, gla_w_out,
              mla_w_in, mla_q_norm, mla_kv_norm, mla_w_uq, mla_w_ukv, mla_w_out,
              final_norm):
    inv_freq = 1.0 / (ROPE_THETA ** (jnp.arange(0, MLA_ROPE, 2, dtype=jnp.float32) / MLA_ROPE))
    ang = positions.astype(jnp.float32)[..., None] * inv_freq
    cos, sin = jnp.cos(ang), jnp.sin(ang)

    for i in range(DEPTH):
        x = x + 0.5 * swiglu(rms_norm(x, ffn_norm[i, 0]), ffn_w_gu[i, 0], ffn_w_down[i, 0])
        h = rms_norm(x, mix_norm[i])
        j = i // N_MIXERS
        if i % N_MIXERS == 0:
            x = x + gla_mixer(h, gla_w_in[j], gla_w_gate2[j], gla_b_gate[j],
                              gla_head_norm[j], gla_w_out[j])
        else:
            x = x + mla_mixer(h, cos, sin, mla_w_in[j], mla_q_norm[j], mla_kv_norm[j],
                              mla_w_uq[j], mla_w_ukv[j], mla_w_out[j])
        x = x + 0.5 * swiglu(rms_norm(x, ffn_norm[i, 1]), ffn_w_gu[i, 1], ffn_w_down[i, 1])
    return rms_norm(x, final_norm)
```

```python
import functools
import math

import numpy as np
import jax
import jax.numpy as jnp
from jax import lax
from jax.experimental import pallas as pl
from jax.experimental.pallas import tpu as pltpu

F32 = jnp.float32
BF16 = jnp.bfloat16

D_MODEL = 1024
BATCH = 8
SEQ = 2048
DEPTH = 4
EPS = 1e-6
D_FF = 2816

GLA_HEADS = 4
GLA_DK_TOT = 512
GLA_DV_TOT = 1024
GLA_DK = 128
GLA_DV = 256
GLA_GATE_RANK = 16
GLA_TAU = 16.0

MLA_HEADS = 8
MLA_NOPE = 128
MLA_ROPE = 64
MLA_V = 128
MLA_Q_RANK = 768
MLA_KV_RANK = 256
MLA_QK = MLA_NOPE + MLA_ROPE
ROPE_THETA = 10000.0

LANES = 128
VMEM_LIMIT_BYTES = 56 * 1024 * 1024

TOKEN_TILE = 512
FFN_CHUNK = 256
GLA_CHUNK = 64
GLA_LEVELS = int(math.log2(GLA_CHUNK))
ATTN_Q_TILE = 512
MLA_HEAD_SLOT = 2 * LANES


def _params(n_axes):
    return pltpu.CompilerParams(
        dimension_semantics=("parallel",) * n_axes,
        vmem_limit_bytes=VMEM_LIMIT_BYTES)


def _dot(a, b):
    return jnp.dot(a, b, preferred_element_type=F32)


def _dot_nt(a, b):
    return lax.dot_general(a, b, (((1,), (1,)), ((), ())), preferred_element_type=F32)


def _dot_tn(a, b):
    return lax.dot_general(a, b, (((0,), (0,)), ((), ())), preferred_element_type=F32)


def _rms(x, g):
    return x * lax.rsqrt(jnp.mean(x * x, axis=-1, keepdims=True) + EPS) * g


def _row_spec(tile, width):
    return pl.BlockSpec((tile, width), lambda i: (i, 0))


def _const_spec(shape):
    return pl.BlockSpec(shape, lambda *_: (0,) * len(shape))


def _ffn_kernel(x_ref, g_ref, wgu_ref, wd_ref, *rest, final):
    o_ref = rest[-1]
    x = x_ref[...]
    h = _rms(x, g_ref[...]).astype(BF16)
    acc = None
    for c in range(D_FF // FFN_CHUNK):
        lo = c * FFN_CHUNK
        gate = _dot(h, wgu_ref[:, lo:lo + FFN_CHUNK])
        up = _dot(h, wgu_ref[:, D_FF + lo:D_FF + lo + FFN_CHUNK])
        act = (gate * jax.nn.sigmoid(gate) * up).astype(BF16)
        part = _dot(act, wd_ref[lo:lo + FFN_CHUNK, :])
        acc = part if acc is None else acc + part
    y = x + 0.5 * acc
    if final:
        y = _rms(y, rest[0][...])
    o_ref[...] = y


def _ffn(x, g, wgu, wd, g_final=None):
    t = x.shape[0]
    final = g_final is not None
    in_specs = [_row_spec(TOKEN_TILE, D_MODEL), _const_spec((1, D_MODEL)),
                _const_spec((D_MODEL, 2 * D_FF)), _const_spec((D_FF, D_MODEL))]
    args = [x, g.reshape(1, D_MODEL), wgu, wd]
    if final:
        in_specs.append(_const_spec((1, D_MODEL)))
        args.append(g_final.reshape(1, D_MODEL))
    return pl.pallas_call(
        functools.partial(_ffn_kernel, final=final),
        grid=(t // TOKEN_TILE,),
        in_specs=in_specs,
        out_specs=_row_spec(TOKEN_TILE, D_MODEL),
        out_shape=jax.ShapeDtypeStruct((t, D_MODEL), F32),
        compiler_params=_params(1),
        name="ffn_final" if final else "ffn",
    )(*args)


def _gla_in_kernel(x_ref, g_ref, w_ref, wg_ref, w2_ref, b2_ref,
                   q_ref, k_ref, v_ref, r_ref, lf_ref, lb_ref):
    h = _rms(x_ref[...], g_ref[...]).astype(BF16)
    y = _dot(h, w_ref[...])
    q_ref[...] = y[:, :GLA_DK_TOT] * (GLA_DK ** -0.5)
    k_ref[...] = y[:, GLA_DK_TOT:2 * GLA_DK_TOT]
    v_ref[...] = y[:, 2 * GLA_DK_TOT:2 * GLA_DK_TOT + GLA_DV_TOT].astype(BF16)
    r_ref[...] = y[:, 2 * GLA_DK_TOT + GLA_DV_TOT:]
    low = _dot(h, wg_ref[...]).astype(BF16)
    z = _dot(low, w2_ref[...]) + b2_ref[...]
    log_a = (jnp.minimum(z, 0.0) - jnp.log1p(jnp.exp(-jnp.abs(z)))) * (1.0 / GLA_TAU)
    lf_ref[...] = log_a[:, :GLA_DK_TOT]
    lb_ref[...] = log_a[:, GLA_DK_TOT:]


def _gla_in(x, g, w_main, w_gate1, w_gate2, b_gate):
    t = x.shape[0]
    n_main = 2 * GLA_DK_TOT + 2 * GLA_DV_TOT
    out_shape = (
        jax.ShapeDtypeStruct((t, GLA_DK_TOT), F32), jax.ShapeDtypeStruct((t, GLA_DK_TOT), F32),
        jax.ShapeDtypeStruct((t, GLA_DV_TOT), BF16), jax.ShapeDtypeStruct((t, GLA_DV_TOT), F32),
        jax.ShapeDtypeStruct((t, GLA_DK_TOT), F32), jax.ShapeDtypeStruct((t, GLA_DK_TOT), F32))
    return pl.pallas_call(
        _gla_in_kernel,
        grid=(t // TOKEN_TILE,),
        in_specs=[_row_spec(TOKEN_TILE, D_MODEL), _const_spec((1, D_MODEL)),
                  _const_spec((D_MODEL, n_main)), _const_spec((D_MODEL, 2 * GLA_GATE_RANK)),
                  _const_spec((2 * GLA_GATE_RANK, 2 * GLA_DK_TOT)), _const_spec((1, 2 * GLA_DK_TOT))],
        out_specs=tuple(_row_spec(TOKEN_TILE, s.shape[1]) for s in out_shape),
        out_shape=out_shape,
        compiler_params=_params(1),
        name="gla_in",
    )(x, g.reshape(1, D_MODEL), w_main, w_gate1, w_gate2, b_gate)


def _gla_constants():
    ch, nl = GLA_CHUNK, GLA_LEVELS
    fw = np.zeros((nl + 2, ch, ch), np.float32)
    bw = np.zeros((nl + 2, ch, ch), np.float32)
    mask = np.zeros((2 * nl + 1, ch, ch), np.float32)
    for lvl in range(nl):
        w = 1 << lvl
        for t in range(ch):
            m = (t // (2 * w)) * 2 * w + w
            if t >= m:
                fw[lvl, t, m + 1:t + 1] = 1.0
                bw[lvl, t, m:t] = 1.0
            else:
                fw[lvl, t, t + 1:m + 1] = 1.0
                bw[lvl, t, t:m] = 1.0
        for t in range(ch):
            for s in range(ch):
                if t // (2 * w) == s // (2 * w) and t % (2 * w) >= w > s % (2 * w):
                    mask[lvl, t, s] = 1.0
                    mask[nl + 1 + lvl, s, t] = 1.0
    mask[nl] = np.eye(ch, dtype=np.float32)
    for t in range(ch):
        fw[nl, t, :t + 1] = 1.0
        fw[nl + 1, t, t + 1:] = 1.0
        bw[nl, t, t:] = 1.0
        bw[nl + 1, t, :t] = 1.0
    ones = np.ones((8, ch), np.float32)
    fw = np.concatenate([fw.reshape(-1, ch), ones], axis=0)
    bw = np.concatenate([bw.reshape(-1, ch), ones], axis=0)
    return jnp.asarray(fw, BF16), jnp.asarray(bw, BF16), jnp.asarray(mask, F32)


def _gla_core_kernel(q_ref, k_ref, v_ref, lf_ref, lb_ref, mf_ref, mb_ref, mask_ref,
                     o_ref, sf_ref, sb_ref):
    ch, nl = GLA_CHUNK, GLA_LEVELS
    n_chunks = SEQ // ch
    sf_ref[...] = jnp.zeros_like(sf_ref)
    sb_ref[...] = jnp.zeros_like(sb_ref)
    o_ref[...] = jnp.zeros_like(o_ref)

    def one_direction(row0, la_ref, m_ref, st_ref, forward):
        rows = pl.ds(row0, ch)
        q = q_ref[rows, :]
        k = k_ref[rows, :]
        v = v_ref[rows, :]
        la = la_ref[rows, :]
        la_hi = la.astype(BF16)
        la_lo = (la - la_hi.astype(F32)).astype(BF16)
        m = m_ref[...]
        x = jnp.exp(_dot(m, la_hi) + _dot(m, la_lo))
        p = None
        for lvl in range(nl):
            xl = x[lvl * ch:(lvl + 1) * ch]
            sc = _dot_nt((q * xl).astype(BF16), (k * xl).astype(BF16))
            sc = sc * mask_ref[lvl if forward else nl + 1 + lvl]
            p = sc if p is None else p + sc
        if forward:
            p = p + _dot_nt(q.astype(BF16), k.astype(BF16)) * mask_ref[nl]
        qd = (q * x[nl * ch:(nl + 1) * ch]).astype(BF16)
        kd = (k * x[(nl + 1) * ch:(nl + 2) * ch]).astype(BF16)
        decay = x[(nl + 2) * ch:(nl + 2) * ch + 1]
        st = st_ref[...]
        o = _dot(p.astype(BF16), v) + _dot_nt(qd, st.astype(BF16))
        st_ref[...] = st * decay + _dot_tn(v, kd)
        o_ref[rows, :] += o

    def body(c, carry):
        one_direction(pl.multiple_of(c * ch, ch), lf_ref, mf_ref, sf_ref, True)
        one_direction(pl.multiple_of((n_chunks - 1 - c) * ch, ch), lb_ref, mb_ref, sb_ref, False)
        return carry

    lax.fori_loop(0, n_chunks, body, 0)


def _gla_core(q, k, v, lf, lb):
    t = q.shape[0]
    m_fw, m_bw, mask = _gla_constants()
    rows = m_fw.shape[0]
    seq_blk = lambda width: pl.BlockSpec((SEQ, width), lambda b, h: (b, h))
    return pl.pallas_call(
        _gla_core_kernel,
        grid=(BATCH, GLA_HEADS),
        in_specs=[seq_blk(GLA_DK), seq_blk(GLA_DK), seq_blk(GLA_DV), seq_blk(GLA_DK), seq_blk(GLA_DK),
                  _const_spec((rows, GLA_CHUNK)), _const_spec((rows, GLA_CHUNK)),
                  _const_spec(mask.shape)],
        out_specs=seq_blk(GLA_DV),
        out_shape=jax.ShapeDtypeStruct((t, GLA_DV_TOT), F32),
        scratch_shapes=[pltpu.VMEM((GLA_DV, GLA_DK), F32), pltpu.VMEM((GLA_DV, GLA_DK), F32)],
        compiler_params=_params(2),
        name="gla_core",
    )(q, k, v, lf, lb, m_fw, m_bw, mask)


def _gla_out_kernel(o_ref, r_ref, x_ref, g_ref, w_ref, y_ref):
    g = g_ref[...]
    parts = []
    for hd in range(GLA_HEADS):
        cols = slice(hd * GLA_DV, (hd + 1) * GLA_DV)
        r = r_ref[:, cols]
        parts.append((_rms(o_ref[:, cols], g) * (r * jax.nn.sigmoid(r))).astype(BF16))
    y_ref[...] = x_ref[...] + _dot(jnp.concatenate(parts, axis=-1), w_ref[...])


def _gla_out(o, r, x, g, w_out):
    t = x.shape[0]
    return pl.pallas_call(
        _gla_out_kernel,
        grid=(t // TOKEN_TILE,),
        in_specs=[_row_spec(TOKEN_TILE, GLA_DV_TOT), _row_spec(TOKEN_TILE, GLA_DV_TOT),
                  _row_spec(TOKEN_TILE, D_MODEL), _const_spec((1, GLA_DV)),
                  _const_spec((GLA_DV_TOT, D_MODEL))],
        out_specs=_row_spec(TOKEN_TILE, D_MODEL),
        out_shape=jax.ShapeDtypeStruct((t, D_MODEL), F32),
        compiler_params=_params(1),
        name="gla_out",
    )(o, r, x, g.reshape(1, GLA_DV), w_out)


def _rope_table_kernel(pos_ref, freq_ref, cos_ref, sin_ref):
    ang = pos_ref[...].astype(F32) * freq_ref[...]
    cos_ref[...] = jnp.cos(ang)
    sin_ref[...] = jnp.sin(ang)


def _rope_tables(positions):
    half = MLA_ROPE // 2
    per_row = LANES // half
    t = positions.size
    inv_freq = 1.0 / (ROPE_THETA ** (jnp.arange(0, MLA_ROPE, 2, dtype=F32) / MLA_ROPE))
    pos = jnp.repeat(positions.reshape(t // per_row, per_row), half, axis=1)
    rows = t // per_row
    tile = 512
    cos, sin = pl.pallas_call(
        _rope_table_kernel,
        grid=(rows // tile,),
        in_specs=[_row_spec(tile, LANES), _const_spec((1, LANES))],
        out_specs=(_row_spec(tile, LANES), _row_spec(tile, LANES)),
        out_shape=(jax.ShapeDtypeStruct((rows, LANES), F32),) * 2,
        compiler_params=_params(1),
        name="rope_table",
    )(pos, jnp.tile(inv_freq, per_row).reshape(1, LANES))
    widen = lambda a: jnp.tile(a.reshape(t, half), (1, per_row))
    return widen(cos), widen(sin)


def _mla_in_kernel(x_ref, g_ref, wcq_ref, wckv_ref, wkpe_ref, wkper_ref, gq_ref, gkv_ref,
                   wqn_ref, wqp_ref, wqpr_ref, wkn_ref, wv_ref, cos_ref, sin_ref,
                   q_out, k_out, v_out):
    h = _rms(x_ref[...], g_ref[...]).astype(BF16)
    cos = cos_ref[...]
    sin = sin_ref[...]
    kpe = (_dot(h, wkpe_ref[...]) * cos + _dot(h, wkper_ref[...]) * sin).astype(BF16)
    hq = _rms(_dot(h, wcq_ref[...]), gq_ref[...]).astype(BF16)
    hkv = _rms(_dot(h, wckv_ref[...]), gkv_ref[...]).astype(BF16)
    scale = MLA_QK ** -0.5
    qn = _dot(hq, wqn_ref[...])
    cos8 = jnp.concatenate([cos] * MLA_HEADS, axis=-1)
    sin8 = jnp.concatenate([sin] * MLA_HEADS, axis=-1)
    qp = _dot(hq, wqp_ref[...]) * cos8 + _dot(hq, wqpr_ref[...]) * sin8
    kn = _dot(hkv, wkn_ref[...])
    v_out[...] = _dot(hkv, wv_ref[...]).astype(BF16)
    for hd in range(MLA_HEADS):
        cols = slice(hd * LANES, (hd + 1) * LANES)
        base = hd * MLA_HEAD_SLOT
        q_out[:, base:base + LANES] = (qn[:, cols] * scale).astype(BF16)
        q_out[:, base + LANES:base + 2 * LANES] = (qp[:, cols] * scale).astype(BF16)
        k_out[:, base:base + LANES] = kn[:, cols].astype(BF16)
        k_out[:, base + LANES:base + 2 * LANES] = kpe


def _mla_in(x, g, w, cos, sin):
    t = x.shape[0]
    weights = [w["wcq"], w["wckv"], w["wkpe"], w["wkper"], w["gq"], w["gkv"],
               w["wqn"], w["wqp"], w["wqpr"], w["wkn"], w["wv"]]
    width = MLA_HEADS * MLA_HEAD_SLOT
    out_shape = (jax.ShapeDtypeStruct((t, width), BF16), jax.ShapeDtypeStruct((t, width), BF16),
                 jax.ShapeDtypeStruct((t, MLA_HEADS * MLA_V), BF16))
    return pl.pallas_call(
        _mla_in_kernel,
        grid=(t // TOKEN_TILE,),
        in_specs=[_row_spec(TOKEN_TILE, D_MODEL), _const_spec((1, D_MODEL))]
                 + [_const_spec(a.shape) for a in weights]
                 + [_row_spec(TOKEN_TILE, LANES), _row_spec(TOKEN_TILE, LANES)],
        out_specs=tuple(_row_spec(TOKEN_TILE, s.shape[1]) for s in out_shape),
        out_shape=out_shape,
        compiler_params=_params(1),
        name="mla_in",
    )(x, g.reshape(1, D_MODEL), *weights, cos, sin)


def _mla_attn_kernel(q_ref, k_ref, v_ref, o_ref):
    s = _dot_nt(q_ref[...], k_ref[...])
    p = jnp.exp(s - jnp.max(s, axis=-1, keepdims=True))
    denom = jnp.sum(p, axis=-1, keepdims=True)
    o_ref[...] = (_dot(p.astype(BF16), v_ref[...]) / denom).astype(o_ref.dtype)


def _mla_attn(q, k, v):
    t = q.shape[0]
    n_q = SEQ // ATTN_Q_TILE
    return pl.pallas_call(
        _mla_attn_kernel,
        grid=(BATCH, MLA_HEADS, n_q),
        in_specs=[pl.BlockSpec((ATTN_Q_TILE, MLA_HEAD_SLOT), lambda b, h, i: (b * n_q + i, h)),
                  pl.BlockSpec((SEQ, MLA_HEAD_SLOT), lambda b, h, i: (b, h)),
                  pl.BlockSpec((SEQ, MLA_V), lambda b, h, i: (b, h))],
        out_specs=pl.BlockSpec((ATTN_Q_TILE, MLA_V), lambda b, h, i: (b * n_q + i, h)),
        out_shape=jax.ShapeDtypeStruct((t, MLA_HEADS * MLA_V), BF16),
        compiler_params=_params(3),
        name="mla_attn",
    )(q, k, v)


def _proj_residual_kernel(a_ref, x_ref, w_ref, y_ref):
    y_ref[...] = x_ref[...] + _dot(a_ref[...], w_ref[...])


def _mla_out(a, x, w_out):
    t = x.shape[0]
    return pl.pallas_call(
        _proj_residual_kernel,
        grid=(t // TOKEN_TILE,),
        in_specs=[_row_spec(TOKEN_TILE, a.shape[1]), _row_spec(TOKEN_TILE, D_MODEL),
                  _const_spec(w_out.shape)],
        out_specs=_row_spec(TOKEN_TILE, D_MODEL),
        out_shape=jax.ShapeDtypeStruct((t, D_MODEL), F32),
        compiler_params=_params(1),
        name="mla_out",
    )(a, x, w_out)


def _rotate_half_columns(w):
    shape = w.shape
    w = w.reshape(shape[0], -1, 2, MLA_ROPE // 2)
    return jnp.stack([-w[:, :, 1], w[:, :, 0]], axis=2).reshape(shape)


def _pad_rope_columns(w):
    k = w.shape[0]
    w = w.reshape(k, -1, MLA_ROPE)
    return jnp.pad(w, ((0, 0), (0, 0), (0, LANES - MLA_ROPE))).reshape(k, -1)


def _mla_weights(w_in, q_norm, kv_norm, w_uq, w_ukv):
    w_kpe = w_in[:, MLA_Q_RANK + MLA_KV_RANK:]
    w_uq = w_uq.reshape(MLA_Q_RANK, MLA_HEADS, MLA_QK)
    w_uq_rope = w_uq[:, :, MLA_NOPE:].reshape(MLA_Q_RANK, MLA_HEADS * MLA_ROPE)
    w_ukv = w_ukv.reshape(MLA_KV_RANK, MLA_HEADS, MLA_NOPE + MLA_V)
    return {
        "wcq": w_in[:, :MLA_Q_RANK].astype(BF16),
        "wckv": w_in[:, MLA_Q_RANK:MLA_Q_RANK + MLA_KV_RANK].astype(BF16),
        "wkpe": _pad_rope_columns(w_kpe).astype(BF16),
        "wkper": _pad_rope_columns(_rotate_half_columns(w_kpe)).astype(BF16),
        "gq": q_norm.reshape(1, MLA_Q_RANK),
        "gkv": kv_norm.reshape(1, MLA_KV_RANK),
        "wqn": w_uq[:, :, :MLA_NOPE].reshape(MLA_Q_RANK, MLA_HEADS * MLA_NOPE).astype(BF16),
        "wqp": _pad_rope_columns(w_uq_rope).astype(BF16),
        "wqpr": _pad_rope_columns(_rotate_half_columns(w_uq_rope)).astype(BF16),
        "wkn": w_ukv[:, :, :MLA_NOPE].reshape(MLA_KV_RANK, MLA_HEADS * MLA_NOPE).astype(BF16),
        "wv": w_ukv[:, :, MLA_NOPE:].reshape(MLA_KV_RANK, MLA_HEADS * MLA_V).astype(BF16),
    }


def kernel(x, positions, ffn_norm, ffn_w_gu, ffn_w_down, mix_norm, gla_w_in, gla_w_gate2, gla_b_gate,
           gla_head_norm, gla_w_out, mla_w_in, mla_q_norm, mla_kv_norm, mla_w_uq, mla_w_ukv, mla_w_out,
           final_norm):
    assert x.shape == (BATCH, SEQ, D_MODEL)
    t = BATCH * SEQ
    x = x.reshape(t, D_MODEL)
    cos, sin = _rope_tables(positions)
    n_main = 2 * GLA_DK_TOT + 2 * GLA_DV_TOT
    zeros = jnp.zeros((GLA_GATE_RANK, GLA_DK_TOT), F32)
    for i in range(DEPTH):
        x = _ffn(x, ffn_norm[i, 0], ffn_w_gu[i, 0].astype(BF16), ffn_w_down[i, 0].astype(BF16))
        j = i // 2
        if i % 2 == 0:
            w_gate2 = jnp.block([[gla_w_gate2[j, 0], zeros], [zeros, gla_w_gate2[j, 1]]]).astype(BF16)
            q, k, v, r, lf, lb = _gla_in(
                x, mix_norm[i], gla_w_in[j, :, :n_main].astype(BF16), gla_w_in[j, :, n_main:].astype(BF16),
                w_gate2, gla_b_gate[j].reshape(1, 2 * GLA_DK_TOT))
            o = _gla_core(q, k, v, lf, lb)
            x = _gla_out(o, r, x, gla_head_norm[j], gla_w_out[j].astype(BF16))
        else:
            w = _mla_weights(mla_w_in[j], mla_q_norm[j], mla_kv_norm[j], mla_w_uq[j], mla_w_ukv[j])
            qf, kf, vf = _mla_in(x, mix_norm[i], w, cos, sin)
            a = _mla_attn(qf, kf, vf)
            x = _mla_out(a, x, mla_w_out[j].astype(BF16))
        x = _ffn(x, ffn_norm[i, 1], ffn_w_gu[i, 1].astype(BF16), ffn_w_down[i, 1].astype(BF16),
                 g_final=final_norm if i == DEPTH - 1 else None)
    return x.reshape(BATCH, SEQ, D_MODEL)
```

```python
import functools
import math

import numpy as np
import jax
import jax.numpy as jnp
from jax import lax
from jax.experimental import pallas as pl
from jax.experimental.pallas import tpu as pltpu

F32 = jnp.float32
BF16 = jnp.bfloat16

D_MODEL = 1024
BATCH = 8
SEQ = 2048
DEPTH = 4
EPS = 1e-6
D_FF = 2816

GLA_HEADS = 4
GLA_DK_TOT = 512
GLA_DV_TOT = 1024
GLA_DK = 128
GLA_DV = 256
GLA_GATE_RANK = 16
GLA_TAU = 16.0

MLA_HEADS = 8
MLA_NOPE = 128
MLA_ROPE = 64
MLA_V = 128
MLA_Q_RANK = 768
MLA_KV_RANK = 256
MLA_QK = MLA_NOPE + MLA_ROPE
ROPE_THETA = 10000.0

LANES = 128
SUBLANES = 8
VMEM_LIMIT_BYTES = 56 * 1024 * 1024

TOKEN_TILE = 512
FFN_CHUNK = 256
GLA_CHUNK = 64
GLA_LEVELS = int(math.log2(GLA_CHUNK))
GLA_SUPER = 256
ATTN_Q_TILE = 512
ATTN_KEY_BLOCK = 512
MLA_HEAD_SLOT = 2 * LANES


def _params(n_axes):
    return pltpu.CompilerParams(
        dimension_semantics=("parallel",) * n_axes,
        vmem_limit_bytes=VMEM_LIMIT_BYTES)


def _dot(a, b):
    return jnp.dot(a, b, preferred_element_type=F32)


def _dot_nt(a, b):
    return lax.dot_general(a, b, (((1,), (1,)), ((), ())), preferred_element_type=F32)


def _dot_tn(a, b):
    return lax.dot_general(a, b, (((0,), (0,)), ((), ())), preferred_element_type=F32)


def _rms(x, g):
    return x * lax.rsqrt(jnp.mean(x * x, axis=-1, keepdims=True) + EPS) * g


def _row_spec(tile, width):
    return pl.BlockSpec((tile, width), lambda i: (i, 0))


def _const_spec(shape):
    return pl.BlockSpec(shape, lambda *_: (0,) * len(shape))


def _ffn_kernel(x_ref, g_ref, wgu_ref, wd_ref, *rest, final):
    o_ref = rest[-1]
    x = x_ref[...]
    h = _rms(x, g_ref[...]).astype(BF16)
    acc = None
    for c in range(D_FF // FFN_CHUNK):
        lo = c * FFN_CHUNK
        gate = _dot(h, wgu_ref[:, lo:lo + FFN_CHUNK])
        up = _dot(h, wgu_ref[:, D_FF + lo:D_FF + lo + FFN_CHUNK])
        act = (gate * jax.nn.sigmoid(gate) * up).astype(BF16)
        part = _dot(act, wd_ref[lo:lo + FFN_CHUNK, :])
        acc = part if acc is None else acc + part
    y = x + 0.5 * acc
    if final:
        y = _rms(y, rest[0][...])
    o_ref[...] = y


def _ffn(x, g, wgu, wd, g_final=None):
    t = x.shape[0]
    final = g_final is not None
    in_specs = [_row_spec(TOKEN_TILE, D_MODEL), _const_spec((1, D_MODEL)),
                _const_spec((D_MODEL, 2 * D_FF)), _const_spec((D_FF, D_MODEL))]
    args = [x, g.reshape(1, D_MODEL), wgu, wd]
    if final:
        in_specs.append(_const_spec((1, D_MODEL)))
        args.append(g_final.reshape(1, D_MODEL))
    return pl.pallas_call(
        functools.partial(_ffn_kernel, final=final),
        grid=(t // TOKEN_TILE,),
        in_specs=in_specs,
        out_specs=_row_spec(TOKEN_TILE, D_MODEL),
        out_shape=jax.ShapeDtypeStruct((t, D_MODEL), F32),
        compiler_params=_params(1),
        name="ffn_final" if final else "ffn",
    )(*args)


def _gla_in_kernel(x_ref, g_ref, w_ref, wg_ref, w2_ref, b2_ref,
                   q_ref, k_ref, v_ref, r_ref, lf_ref, lb_ref):
    h = _rms(x_ref[...], g_ref[...]).astype(BF16)
    y = _dot(h, w_ref[...])
    q_ref[...] = y[:, :GLA_DK_TOT] * (GLA_DK ** -0.5)
    k_ref[...] = y[:, GLA_DK_TOT:2 * GLA_DK_TOT]
    v_ref[...] = y[:, 2 * GLA_DK_TOT:2 * GLA_DK_TOT + GLA_DV_TOT].astype(BF16)
    r_ref[...] = y[:, 2 * GLA_DK_TOT + GLA_DV_TOT:]
    low = _dot(h, wg_ref[...]).astype(BF16)
    z = _dot(low, w2_ref[...]) + b2_ref[...]
    log_a = (jnp.minimum(z, 0.0) - jnp.log1p(jnp.exp(-jnp.abs(z)))) * (1.0 / GLA_TAU)
    lf_ref[...] = log_a[:, :GLA_DK_TOT]
    lb_ref[...] = log_a[:, GLA_DK_TOT:]


def _gla_in(x, g, w_main, w_gate1, w_gate2, b_gate):
    t = x.shape[0]
    n_main = 2 * GLA_DK_TOT + 2 * GLA_DV_TOT
    out_shape = (
        jax.ShapeDtypeStruct((t, GLA_DK_TOT), F32), jax.ShapeDtypeStruct((t, GLA_DK_TOT), F32),
        jax.ShapeDtypeStruct((t, GLA_DV_TOT), BF16), jax.ShapeDtypeStruct((t, GLA_DV_TOT), F32),
        jax.ShapeDtypeStruct((t, GLA_DK_TOT), F32), jax.ShapeDtypeStruct((t, GLA_DK_TOT), F32))
    return pl.pallas_call(
        _gla_in_kernel,
        grid=(t // TOKEN_TILE,),
        in_specs=[_row_spec(TOKEN_TILE, D_MODEL), _const_spec((1, D_MODEL)),
                  _const_spec((D_MODEL, n_main)), _const_spec((D_MODEL, 2 * GLA_GATE_RANK)),
                  _const_spec((2 * GLA_GATE_RANK, 2 * GLA_DK_TOT)), _const_spec((1, 2 * GLA_DK_TOT))],
        out_specs=tuple(_row_spec(TOKEN_TILE, s.shape[1]) for s in out_shape),
        out_shape=out_shape,
        compiler_params=_params(1),
        name="gla_in",
    )(x, g.reshape(1, D_MODEL), w_main, w_gate1, w_gate2, b_gate)


def _gla_constants():
    ch, nl, sup = GLA_CHUNK, GLA_LEVELS, GLA_SUPER
    tri = np.tril(np.ones((ch, ch), np.float32))
    eye = np.eye(sup // ch, dtype=np.float32)
    cum_fw = np.kron(eye, tri)
    cum_bw = np.kron(eye, tri.T)
    mask = np.zeros((2 * nl + 1, ch, ch), np.float32)
    t = np.arange(ch)[:, None]
    s = np.arange(ch)[None, :]
    for lvl in range(nl):
        w = 1 << lvl
        sel = (t // (2 * w) == s // (2 * w)) & (t % (2 * w) >= w) & (s % (2 * w) < w)
        mask[lvl] = sel
        mask[nl + 1 + lvl] = sel.T
    mask[nl] = np.eye(ch, dtype=np.float32)
    return jnp.asarray(cum_fw, BF16), jnp.asarray(cum_bw, BF16), jnp.asarray(mask, F32)


def _level_reference(b, lvl):
    ch = GLA_CHUNK
    w = 1 << lvl
    if 2 * w >= 2 * SUBLANES:
        parts = [jnp.broadcast_to(b[blk + w:blk + w + 1, :], (2 * w, LANES))
                 for blk in range(0, ch, 2 * w)]
        return parts[0] if len(parts) == 1 else jnp.concatenate(parts, axis=0)
    b3 = b.reshape(ch // SUBLANES, SUBLANES, LANES)
    sub = lax.broadcasted_iota(jnp.int32, b3.shape, 1)
    ref = None
    for blk in range(SUBLANES - 2 * w, -1, -2 * w):
        row = jnp.broadcast_to(b3[:, blk + w:blk + w + 1, :], b3.shape)
        ref = row if ref is None else jnp.where(sub < blk + 2 * w, row, ref)
    return ref.reshape(ch, LANES)


def _gla_core_kernel(q_ref, k_ref, v_ref, lf_ref, lb_ref, cf_ref, cb_ref, mask_ref,
                     o_ref, sf_ref, sb_ref):
    ch, nl, sup = GLA_CHUNK, GLA_LEVELS, GLA_SUPER
    n_super = SEQ // sup
    sf_ref[...] = jnp.zeros_like(sf_ref)
    sb_ref[...] = jnp.zeros_like(sb_ref)
    o_ref[...] = jnp.zeros_like(o_ref)

    def chunk(row0, j, cum, st_ref, forward):
        rows = pl.ds(row0 + j * ch, ch)
        q = q_ref[rows, :]
        k = k_ref[rows, :]
        v = v_ref[rows, :]
        b = cum[j * ch:(j + 1) * ch]
        p = None
        for lvl in range(nl):
            x = jnp.exp(-jnp.abs(b - _level_reference(b, lvl)))
            sc = _dot_nt((q * x).astype(BF16), (k * x).astype(BF16))
            sc = sc * mask_ref[lvl if forward else nl + 1 + lvl]
            p = sc if p is None else p + sc
        if forward:
            p = p + _dot_nt(q.astype(BF16), k.astype(BF16)) * mask_ref[nl]
        edge = b[ch - 1:ch, :] if forward else b[0:1, :]
        qd = (q * jnp.exp(b)).astype(BF16)
        kd = (k * jnp.exp(edge - b)).astype(BF16)
        decay = jnp.broadcast_to(jnp.exp(edge), (GLA_DK, GLA_DK)).T
        st = st_ref[...]
        o = _dot(p.astype(BF16), v) + _dot(qd, st.astype(BF16))
        st_ref[...] = st * jnp.concatenate([decay, decay], axis=1) + _dot_tn(kd, v)
        o_ref[rows, :] += o

    def direction(row0, la_ref, cum_ref, st_ref, forward):
        la = la_ref[pl.ds(row0, sup), :]
        la_hi = la.astype(BF16)
        la_lo = (la - la_hi.astype(F32)).astype(BF16)
        cum = _dot(cum_ref[...], la_hi) + _dot(cum_ref[...], la_lo)
        order = range(sup // ch) if forward else range(sup // ch - 1, -1, -1)
        for j in order:
            chunk(row0, j, cum, st_ref, forward)

    def body(i, carry):
        direction(pl.multiple_of(i * sup, sup), lf_ref, cf_ref, sf_ref, True)
        direction(pl.multiple_of((n_super - 1 - i) * sup, sup), lb_ref, cb_ref, sb_ref, False)
        return carry

    lax.fori_loop(0, n_super, body, 0)


def _gla_core(q, k, v, lf, lb):
    t = q.shape[0]
    cum_fw, cum_bw, mask = _gla_constants()
    seq_blk = lambda width: pl.BlockSpec((SEQ, width), lambda b, h: (b, h))
    return pl.pallas_call(
        _gla_core_kernel,
        grid=(BATCH, GLA_HEADS),
        in_specs=[seq_blk(GLA_DK), seq_blk(GLA_DK), seq_blk(GLA_DV), seq_blk(GLA_DK), seq_blk(GLA_DK),
                  _const_spec(cum_fw.shape), _const_spec(cum_bw.shape), _const_spec(mask.shape)],
        out_specs=seq_blk(GLA_DV),
        out_shape=jax.ShapeDtypeStruct((t, GLA_DV_TOT), F32),
        scratch_shapes=[pltpu.VMEM((GLA_DK, GLA_DV), F32), pltpu.VMEM((GLA_DK, GLA_DV), F32)],
        compiler_params=_params(2),
        name="gla_core",
    )(q, k, v, lf, lb, cum_fw, cum_bw, mask)


def _gla_out_kernel(o_ref, r_ref, x_ref, g_ref, w_ref, y_ref):
    g = g_ref[...]
    parts = []
    for hd in range(GLA_HEADS):
        cols = slice(hd * GLA_DV, (hd + 1) * GLA_DV)
        r = r_ref[:, cols]
        parts.append((_rms(o_ref[:, cols], g) * (r * jax.nn.sigmoid(r))).astype(BF16))
    y_ref[...] = x_ref[...] + _dot(jnp.concatenate(parts, axis=-1), w_ref[...])


def _gla_out(o, r, x, g, w_out):
    t = x.shape[0]
    return pl.pallas_call(
        _gla_out_kernel,
        grid=(t // TOKEN_TILE,),
        in_specs=[_row_spec(TOKEN_TILE, GLA_DV_TOT), _row_spec(TOKEN_TILE, GLA_DV_TOT),
                  _row_spec(TOKEN_TILE, D_MODEL), _const_spec((1, GLA_DV)),
                  _const_spec((GLA_DV_TOT, D_MODEL))],
        out_specs=_row_spec(TOKEN_TILE, D_MODEL),
        out_shape=jax.ShapeDtypeStruct((t, D_MODEL), F32),
        compiler_params=_params(1),
        name="gla_out",
    )(o, r, x, g.reshape(1, GLA_DV), w_out)


def _rope_table_kernel(pos_ref, freq_ref, cos_ref, sin_ref):
    ang = pos_ref[...].astype(F32) * freq_ref[...]
    cos_ref[...] = jnp.cos(ang)
    sin_ref[...] = jnp.sin(ang)


def _rope_tables(positions):
    half = MLA_ROPE // 2
    per_row = LANES // half
    t = positions.size
    inv_freq = 1.0 / (ROPE_THETA ** (jnp.arange(0, MLA_ROPE, 2, dtype=F32) / MLA_ROPE))
    pos = jnp.repeat(positions.reshape(t // per_row, per_row), half, axis=1)
    rows = t // per_row
    tile = 512
    cos, sin = pl.pallas_call(
        _rope_table_kernel,
        grid=(rows // tile,),
        in_specs=[_row_spec(tile, LANES), _const_spec((1, LANES))],
        out_specs=(_row_spec(tile, LANES), _row_spec(tile, LANES)),
        out_shape=(jax.ShapeDtypeStruct((rows, LANES), F32),) * 2,
        compiler_params=_params(1),
        name="rope_table",
    )(pos, jnp.tile(inv_freq, per_row).reshape(1, LANES))
    widen = lambda a: jnp.tile(a.reshape(t, half), (1, per_row))
    return widen(cos), widen(sin)


def _mla_in_kernel(x_ref, g_ref, wcq_ref, wckv_ref, wkpe_ref, wkper_ref, gq_ref, gkv_ref,
                   wqn_ref, wqp_ref, wqpr_ref, wkn_ref, wv_ref, cos_ref, sin_ref,
                   q_out, k_out, vt_out):
    h = _rms(x_ref[...], g_ref[...]).astype(BF16)
    cos = cos_ref[...]
    sin = sin_ref[...]
    kpe = (_dot(h, wkpe_ref[...]) * cos + _dot(h, wkper_ref[...]) * sin).astype(BF16)
    hq = _rms(_dot(h, wcq_ref[...]), gq_ref[...]).astype(BF16)
    hkv = _rms(_dot(h, wckv_ref[...]), gkv_ref[...]).astype(BF16)
    scale = MLA_QK ** -0.5 * math.log2(math.e)
    qn = _dot(hq, wqn_ref[...])
    cos8 = jnp.concatenate([cos] * MLA_HEADS, axis=-1)
    sin8 = jnp.concatenate([sin] * MLA_HEADS, axis=-1)
    qp = _dot(hq, wqp_ref[...]) * cos8 + _dot(hq, wqpr_ref[...]) * sin8
    kn = _dot(hkv, wkn_ref[...])
    vt_out[...] = _dot(hkv, wv_ref[...]).T.astype(BF16)
    for hd in range(MLA_HEADS):
        cols = slice(hd * LANES, (hd + 1) * LANES)
        base = hd * MLA_HEAD_SLOT
        q_out[:, base:base + LANES] = (qn[:, cols] * scale).astype(BF16)
        q_out[:, base + LANES:base + 2 * LANES] = (qp[:, cols] * scale).astype(BF16)
        k_out[:, base:base + LANES] = kn[:, cols].astype(BF16)
        k_out[:, base + LANES:base + 2 * LANES] = kpe


def _mla_in(x, g, w, cos, sin):
    t = x.shape[0]
    weights = [w["wcq"], w["wckv"], w["wkpe"], w["wkper"], w["gq"], w["gkv"],
               w["wqn"], w["wqp"], w["wqpr"], w["wkn"], w["wv"]]
    width = MLA_HEADS * MLA_HEAD_SLOT
    out_shape = (jax.ShapeDtypeStruct((t, width), BF16), jax.ShapeDtypeStruct((t, width), BF16),
                 jax.ShapeDtypeStruct((MLA_HEADS * MLA_V, t), BF16))
    return pl.pallas_call(
        _mla_in_kernel,
        grid=(t // TOKEN_TILE,),
        in_specs=[_row_spec(TOKEN_TILE, D_MODEL), _const_spec((1, D_MODEL))]
                 + [_const_spec(a.shape) for a in weights]
                 + [_row_spec(TOKEN_TILE, LANES), _row_spec(TOKEN_TILE, LANES)],
        out_specs=(_row_spec(TOKEN_TILE, width), _row_spec(TOKEN_TILE, width),
                   pl.BlockSpec((MLA_HEADS * MLA_V, TOKEN_TILE), lambda i: (0, i))),
        out_shape=out_shape,
        compiler_params=_params(1),
        name="mla_in",
    )(x, g.reshape(1, D_MODEL), *weights, cos, sin)


def _mla_attn_kernel(q_ref, k_ref, vt_ref, o_ref):
    q = q_ref[...]
    m = l = acc = None
    for j in range(SEQ // ATTN_KEY_BLOCK):
        keys = slice(j * ATTN_KEY_BLOCK, (j + 1) * ATTN_KEY_BLOCK)
        s = _dot_nt(k_ref[keys, :], q)
        m_blk = jnp.max(s, axis=0, keepdims=True)
        if j == 0:
            m = m_blk
            p = jnp.exp2(s - m)
            l = jnp.sum(p, axis=0, keepdims=True)
            acc = _dot(vt_ref[:, keys], p.astype(BF16))
        else:
            m_new = jnp.maximum(m, m_blk)
            alpha = jnp.exp2(m - m_new)
            p = jnp.exp2(s - m_new)
            l = alpha * l + jnp.sum(p, axis=0, keepdims=True)
            acc = alpha * acc + _dot(vt_ref[:, keys], p.astype(BF16))
            m = m_new
    o_ref[...] = (acc / l).T.astype(o_ref.dtype)


def _mla_attn(q, k, vt):
    t = q.shape[0]
    n_q = SEQ // ATTN_Q_TILE
    return pl.pallas_call(
        _mla_attn_kernel,
        grid=(BATCH, MLA_HEADS, n_q),
        in_specs=[pl.BlockSpec((ATTN_Q_TILE, MLA_HEAD_SLOT), lambda b, h, i: (b * n_q + i, h)),
                  pl.BlockSpec((SEQ, MLA_HEAD_SLOT), lambda b, h, i: (b, h)),
                  pl.BlockSpec((MLA_V, SEQ), lambda b, h, i: (h, b))],
        out_specs=pl.BlockSpec((ATTN_Q_TILE, MLA_V), lambda b, h, i: (b * n_q + i, h)),
        out_shape=jax.ShapeDtypeStruct((t, MLA_HEADS * MLA_V), BF16),
        compiler_params=_params(3),
        name="mla_attn",
    )(q, k, vt)


def _proj_residual_kernel(a_ref, x_ref, w_ref, y_ref):
    y_ref[...] = x_ref[...] + _dot(a_ref[...], w_ref[...])


def _mla_out(a, x, w_out):
    t = x.shape[0]
    return pl.pallas_call(
        _proj_residual_kernel,
        grid=(t // TOKEN_TILE,),
        in_specs=[_row_spec(TOKEN_TILE, a.shape[1]), _row_spec(TOKEN_TILE, D_MODEL),
                  _const_spec(w_out.shape)],
        out_specs=_row_spec(TOKEN_TILE, D_MODEL),
        out_shape=jax.ShapeDtypeStruct((t, D_MODEL), F32),
        compiler_params=_params(1),
        name="mla_out",
    )(a, x, w_out)


def _rotate_half_columns(w):
    shape = w.shape
    w = w.reshape(shape[0], -1, 2, MLA_ROPE // 2)
    return jnp.stack([-w[:, :, 1], w[:, :, 0]], axis=2).reshape(shape)


def _pad_rope_columns(w):
    k = w.shape[0]
    w = w.reshape(k, -1, MLA_ROPE)
    return jnp.pad(w, ((0, 0), (0, 0), (0, LANES - MLA_ROPE))).reshape(k, -1)


def _mla_weights(w_in, q_norm, kv_norm, w_uq, w_ukv):
    w_kpe = w_in[:, MLA_Q_RANK + MLA_KV_RANK:]
    w_uq = w_uq.reshape(MLA_Q_RANK, MLA_HEADS, MLA_QK)
    w_uq_rope = w_uq[:, :, MLA_NOPE:].reshape(MLA_Q_RANK, MLA_HEADS * MLA_ROPE)
    w_ukv = w_ukv.reshape(MLA_KV_RANK, MLA_HEADS, MLA_NOPE + MLA_V)
    return {
        "wcq": w_in[:, :MLA_Q_RANK].astype(BF16),
        "wckv": w_in[:, MLA_Q_RANK:MLA_Q_RANK + MLA_KV_RANK].astype(BF16),
        "wkpe": _pad_rope_columns(w_kpe).astype(BF16),
        "wkper": _pad_rope_columns(_rotate_half_columns(w_kpe)).astype(BF16),
        "gq": q_norm.reshape(1, MLA_Q_RANK),
        "gkv": kv_norm.reshape(1, MLA_KV_RANK),
        "wqn": w_uq[:, :, :MLA_NOPE].reshape(MLA_Q_RANK, MLA_HEADS * MLA_NOPE).astype(BF16),
        "wqp": _pad_rope_columns(w_uq_rope).astype(BF16),
        "wqpr": _pad_rope_columns(_rotate_half_columns(w_uq_rope)).astype(BF16),
        "wkn": w_ukv[:, :, :MLA_NOPE].reshape(MLA_KV_RANK, MLA_HEADS * MLA_NOPE).astype(BF16),
        "wv": w_ukv[:, :, MLA_NOPE:].reshape(MLA_KV_RANK, MLA_HEADS * MLA_V).astype(BF16),
    }


def kernel(x, positions, ffn_norm, ffn_w_gu, ffn_w_down, mix_norm, gla_w_in, gla_w_gate2, gla_b_gate,
           gla_head_norm, gla_w_out, mla_w_in, mla_q_norm, mla_kv_norm, mla_w_uq, mla_w_ukv, mla_w_out,
           final_norm):
    assert x.shape == (BATCH, SEQ, D_MODEL)
    t = BATCH * SEQ
    x = x.reshape(t, D_MODEL)
    cos, sin = _rope_tables(positions)
    n_main = 2 * GLA_DK_TOT + 2 * GLA_DV_TOT
    zeros = jnp.zeros((GLA_GATE_RANK, GLA_DK_TOT), F32)
    for i in range(DEPTH):
        x = _ffn(x, ffn_norm[i, 0], ffn_w_gu[i, 0].astype(BF16), ffn_w_down[i, 0].astype(BF16))
        j = i // 2
        if i % 2 == 0:
            w_gate2 = jnp.block([[gla_w_gate2[j, 0], zeros], [zeros, gla_w_gate2[j, 1]]]).astype(BF16)
            q, k, v, r, lf, lb = _gla_in(
                x, mix_norm[i], gla_w_in[j, :, :n_main].astype(BF16), gla_w_in[j, :, n_main:].astype(BF16),
                w_gate2, gla_b_gate[j].reshape(1, 2 * GLA_DK_TOT))
            o = _gla_core(q, k, v, lf, lb)
            x = _gla_out(o, r, x, gla_head_norm[j], gla_w_out[j].astype(BF16))
        else:
            w = _mla_weights(mla_w_in[j], mla_q_norm[j], mla_kv_norm[j], mla_w_uq[j], mla_w_ukv[j])
            qf, kf, vt = _mla_in(x, mix_norm[i], w, cos, sin)
            a = _mla_attn(qf, kf, vt)
            x = _mla_out(a, x, mla_w_out[j].astype(BF16))
        x = _ffn(x, ffn_norm[i, 1], ffn_w_gu[i, 1].astype(BF16), ffn_w_down[i, 1].astype(BF16),
                 g_final=final_norm if i == DEPTH - 1 else None)
    return x.reshape(BATCH, SEQ, D_MODEL)
```

```python
import functools
import math

import numpy as np
import jax
import jax.numpy as jnp
from jax import lax
from jax.experimental import pallas as pl
from jax.experimental.pallas import tpu as pltpu

F32 = jnp.float32
BF16 = jnp.bfloat16

D_MODEL = 1024
BATCH = 8
SEQ = 2048
DEPTH = 4
EPS = 1e-6
D_FF = 2816

GLA_HEADS = 4
GLA_DK_TOT = 512
GLA_DV_TOT = 1024
GLA_DK = 128
GLA_DV = 256
GLA_GATE_RANK = 16
GLA_TAU = 16.0

MLA_HEADS = 8
MLA_NOPE = 128
MLA_ROPE = 64
MLA_V = 128
MLA_Q_RANK = 768
MLA_KV_RANK = 256
MLA_QK = MLA_NOPE + MLA_ROPE
ROPE_THETA = 10000.0

LANES = 128
SUBLANES = 8
VMEM_LIMIT_BYTES = 56 * 1024 * 1024

TOKEN_TILE = 512
FFN_CHUNK = 256
GLA_CHUNK = 64
GLA_LEVELS = int(math.log2(GLA_CHUNK))
GLA_SUPER = 256
ATTN_Q_TILE = 2048
ATTN_KEY_BLOCK = 512
MLA_HEAD_SLOT = 2 * LANES


def _params(n_axes):
    return pltpu.CompilerParams(
        dimension_semantics=("parallel",) * n_axes,
        vmem_limit_bytes=VMEM_LIMIT_BYTES)


def _dot(a, b):
    return jnp.dot(a, b, preferred_element_type=F32)


def _dot_nt(a, b):
    return lax.dot_general(a, b, (((1,), (1,)), ((), ())), preferred_element_type=F32)


def _dot_tn(a, b):
    return lax.dot_general(a, b, (((0,), (0,)), ((), ())), preferred_element_type=F32)


def _rms(x, g):
    return x * lax.rsqrt(jnp.mean(x * x, axis=-1, keepdims=True) + EPS) * g


def _row_spec(tile, width):
    return pl.BlockSpec((tile, width), lambda i: (i, 0))


def _const_spec(shape):
    return pl.BlockSpec(shape, lambda *_: (0,) * len(shape))


def _ffn_kernel(x_ref, g_ref, wgu_ref, wd_ref, *rest, final):
    o_ref = rest[-1]
    x = x_ref[...]
    h = _rms(x, g_ref[...]).astype(BF16)
    acc = None
    for c in range(D_FF // FFN_CHUNK):
        lo = c * FFN_CHUNK
        gate = _dot(h, wgu_ref[:, lo:lo + FFN_CHUNK])
        up = _dot(h, wgu_ref[:, D_FF + lo:D_FF + lo + FFN_CHUNK])
        act = (gate * jax.nn.sigmoid(gate) * up).astype(BF16)
        part = _dot(act, wd_ref[lo:lo + FFN_CHUNK, :])
        acc = part if acc is None else acc + part
    y = x + 0.5 * acc
    if final:
        y = _rms(y, rest[0][...])
    o_ref[...] = y


def _ffn(x, g, wgu, wd, g_final=None):
    t = x.shape[0]
    final = g_final is not None
    in_specs = [_row_spec(TOKEN_TILE, D_MODEL), _const_spec((1, D_MODEL)),
                _const_spec((D_MODEL, 2 * D_FF)), _const_spec((D_FF, D_MODEL))]
    args = [x, g.reshape(1, D_MODEL), wgu, wd]
    if final:
        in_specs.append(_const_spec((1, D_MODEL)))
        args.append(g_final.reshape(1, D_MODEL))
    return pl.pallas_call(
        functools.partial(_ffn_kernel, final=final),
        grid=(t // TOKEN_TILE,),
        in_specs=in_specs,
        out_specs=_row_spec(TOKEN_TILE, D_MODEL),
        out_shape=jax.ShapeDtypeStruct((t, D_MODEL), F32),
        compiler_params=_params(1),
        name="ffn_final" if final else "ffn",
    )(*args)


def _gla_in_kernel(x_ref, g_ref, w_ref, wg_ref, w2_ref, b2_ref,
                   q_ref, k_ref, v_ref, r_ref, lf_ref, lb_ref):
    h = _rms(x_ref[...], g_ref[...]).astype(BF16)
    y = _dot(h, w_ref[...])
    q_ref[...] = (y[:, :GLA_DK_TOT] * (GLA_DK ** -0.5)).astype(BF16)
    k_ref[...] = y[:, GLA_DK_TOT:2 * GLA_DK_TOT].astype(BF16)
    v_ref[...] = y[:, 2 * GLA_DK_TOT:2 * GLA_DK_TOT + GLA_DV_TOT].astype(BF16)
    r_ref[...] = y[:, 2 * GLA_DK_TOT + GLA_DV_TOT:]
    low = _dot(h, wg_ref[...]).astype(BF16)
    z = _dot(low, w2_ref[...]) + b2_ref[...]
    log_a = (jnp.minimum(z, 0.0) - jnp.log1p(jnp.exp(-jnp.abs(z)))) * (math.log2(math.e) / GLA_TAU)
    lf_ref[...] = log_a[:, :GLA_DK_TOT]
    lb_ref[...] = log_a[:, GLA_DK_TOT:]


def _gla_in(x, g, w_main, w_gate1, w_gate2, b_gate):
    t = x.shape[0]
    n_main = 2 * GLA_DK_TOT + 2 * GLA_DV_TOT
    out_shape = (
        jax.ShapeDtypeStruct((t, GLA_DK_TOT), BF16), jax.ShapeDtypeStruct((t, GLA_DK_TOT), BF16),
        jax.ShapeDtypeStruct((t, GLA_DV_TOT), BF16), jax.ShapeDtypeStruct((t, GLA_DV_TOT), F32),
        jax.ShapeDtypeStruct((t, GLA_DK_TOT), F32), jax.ShapeDtypeStruct((t, GLA_DK_TOT), F32))
    return pl.pallas_call(
        _gla_in_kernel,
        grid=(t // TOKEN_TILE,),
        in_specs=[_row_spec(TOKEN_TILE, D_MODEL), _const_spec((1, D_MODEL)),
                  _const_spec((D_MODEL, n_main)), _const_spec((D_MODEL, 2 * GLA_GATE_RANK)),
                  _const_spec((2 * GLA_GATE_RANK, 2 * GLA_DK_TOT)), _const_spec((1, 2 * GLA_DK_TOT))],
        out_specs=tuple(_row_spec(TOKEN_TILE, s.shape[1]) for s in out_shape),
        out_shape=out_shape,
        compiler_params=_params(1),
        name="gla_in",
    )(x, g.reshape(1, D_MODEL), w_main, w_gate1, w_gate2, b_gate)


def _gla_constants():
    ch, nl, sup = GLA_CHUNK, GLA_LEVELS, GLA_SUPER
    tri = np.tril(np.ones((ch, ch), np.float32))
    eye = np.eye(sup // ch, dtype=np.float32)
    cum_fw = np.kron(eye, tri)
    cum_bw = np.kron(eye, tri.T)
    mask = np.zeros((2 * nl + 1, ch, ch), np.float32)
    t = np.arange(ch)[:, None]
    s = np.arange(ch)[None, :]
    for lvl in range(nl):
        w = 1 << lvl
        sel = (t // (2 * w) == s // (2 * w)) & (t % (2 * w) >= w) & (s % (2 * w) < w)
        mask[lvl] = sel
        mask[nl + 1 + lvl] = sel.T
    mask[nl] = np.eye(ch, dtype=np.float32)
    return jnp.asarray(cum_fw, BF16), jnp.asarray(cum_bw, BF16), jnp.asarray(mask, F32)


def _level_reference(b, lvl):
    ch = GLA_CHUNK
    w = 1 << lvl
    if 2 * w >= 2 * SUBLANES:
        parts = [jnp.broadcast_to(b[blk + w:blk + w + 1, :], (2 * w, LANES))
                 for blk in range(0, ch, 2 * w)]
        return parts[0] if len(parts) == 1 else jnp.concatenate(parts, axis=0)
    b3 = b.reshape(ch // SUBLANES, SUBLANES, LANES)
    sub = lax.broadcasted_iota(jnp.int32, b3.shape, 1)
    ref = None
    for blk in range(SUBLANES - 2 * w, -1, -2 * w):
        row = jnp.broadcast_to(b3[:, blk + w:blk + w + 1, :], b3.shape)
        ref = row if ref is None else jnp.where(sub < blk + 2 * w, row, ref)
    return ref.reshape(ch, LANES)


def _gla_core_kernel(q_ref, k_ref, v_ref, lf_ref, lb_ref, cf_ref, cb_ref, mask_ref,
                     o_ref, sf_ref, sb_ref):
    ch, nl, sup = GLA_CHUNK, GLA_LEVELS, GLA_SUPER
    n_super = SEQ // sup
    sf_ref[...] = jnp.zeros_like(sf_ref)
    sb_ref[...] = jnp.zeros_like(sb_ref)
    o_ref[...] = jnp.zeros_like(o_ref)

    def cumulative(row0, la_ref, cum_ref):
        la = la_ref[pl.ds(row0, sup), :]
        la_hi = la.astype(BF16)
        la_lo = (la - la_hi.astype(F32)).astype(BF16)
        return _dot(cum_ref[...], la_hi) + _dot(cum_ref[...], la_lo)

    def intra(row0, j, cum, forward):
        rows = pl.ds(row0 + j * ch, ch)
        q = q_ref[rows, :]
        k = k_ref[rows, :]
        v = v_ref[rows, :]
        b = cum[j * ch:(j + 1) * ch]
        p = None
        for lvl in range(nl):
            x = jnp.exp2(-jnp.abs(b - _level_reference(b, lvl))).astype(BF16)
            sc = _dot_nt(q * x, k * x) * mask_ref[lvl if forward else nl + 1 + lvl]
            p = sc if p is None else p + sc
        if forward:
            p = p + _dot_nt(q, k) * mask_ref[nl]
        edge = b[ch - 1:ch, :] if forward else b[0:1, :]
        qd = q * jnp.exp2(b).astype(BF16)
        kd = k * jnp.exp2(edge - b).astype(BF16)
        decay = jnp.broadcast_to(jnp.exp2(edge), (GLA_DK, GLA_DK)).T
        return rows, jnp.concatenate([qd, p.astype(BF16)], axis=1), v, decay, _dot_tn(kd, v)

    def inter(st_ref, rows, qd_p, v, decay, update):
        st = st_ref[...]
        o_ref[rows, :] += _dot(qd_p, jnp.concatenate([st.astype(BF16), v], axis=0))
        st_ref[...] = st * jnp.concatenate([decay, decay], axis=1) + update

    def body(i, carry):
        n = sup // ch
        row_f = pl.multiple_of(i * sup, sup)
        row_b = pl.multiple_of((n_super - 1 - i) * sup, sup)
        cum_f = cumulative(row_f, lf_ref, cf_ref)
        cum_b = cumulative(row_b, lb_ref, cb_ref)
        done = []
        for j in range(n):
            done.append((sf_ref, intra(row_f, j, cum_f, True)))
            done.append((sb_ref, intra(row_b, n - 1 - j, cum_b, False)))
        for st_ref, parts in done:
            inter(st_ref, *parts)
        return carry

    lax.fori_loop(0, n_super, body, 0)


def _gla_core(q, k, v, lf, lb):
    t = q.shape[0]
    cum_fw, cum_bw, mask = _gla_constants()
    seq_blk = lambda width: pl.BlockSpec((SEQ, width), lambda b, h: (b, h))
    return pl.pallas_call(
        _gla_core_kernel,
        grid=(BATCH, GLA_HEADS),
        in_specs=[seq_blk(GLA_DK), seq_blk(GLA_DK), seq_blk(GLA_DV), seq_blk(GLA_DK), seq_blk(GLA_DK),
                  _const_spec(cum_fw.shape), _const_spec(cum_bw.shape), _const_spec(mask.shape)],
        out_specs=seq_blk(GLA_DV),
        out_shape=jax.ShapeDtypeStruct((t, GLA_DV_TOT), F32),
        scratch_shapes=[pltpu.VMEM((GLA_DK, GLA_DV), F32), pltpu.VMEM((GLA_DK, GLA_DV), F32)],
        compiler_params=_params(2),
        name="gla_core",
    )(q, k, v, lf, lb, cum_fw, cum_bw, mask)


def _gla_out_kernel(o_ref, r_ref, x_ref, g_ref, w_ref, y_ref):
    g = g_ref[...]
    parts = []
    for hd in range(GLA_HEADS):
        cols = slice(hd * GLA_DV, (hd + 1) * GLA_DV)
        r = r_ref[:, cols]
        parts.append((_rms(o_ref[:, cols], g) * (r * jax.nn.sigmoid(r))).astype(BF16))
    y_ref[...] = x_ref[...] + _dot(jnp.concatenate(parts, axis=-1), w_ref[...])


def _gla_out(o, r, x, g, w_out):
    t = x.shape[0]
    return pl.pallas_call(
        _gla_out_kernel,
        grid=(t // TOKEN_TILE,),
        in_specs=[_row_spec(TOKEN_TILE, GLA_DV_TOT), _row_spec(TOKEN_TILE, GLA_DV_TOT),
                  _row_spec(TOKEN_TILE, D_MODEL), _const_spec((1, GLA_DV)),
                  _const_spec((GLA_DV_TOT, D_MODEL))],
        out_specs=_row_spec(TOKEN_TILE, D_MODEL),
        out_shape=jax.ShapeDtypeStruct((t, D_MODEL), F32),
        compiler_params=_params(1),
        name="gla_out",
    )(o, r, x, g.reshape(1, GLA_DV), w_out)


def _rope_table_kernel(pos_ref, freq_ref, cos_ref, sin_ref):
    ang = pos_ref[...].astype(F32) * freq_ref[...]
    cos_ref[...] = jnp.cos(ang)
    sin_ref[...] = jnp.sin(ang)


def _rope_tables(positions):
    half = MLA_ROPE // 2
    per_row = LANES // half
    t = positions.size
    inv_freq = 1.0 / (ROPE_THETA ** (jnp.arange(0, MLA_ROPE, 2, dtype=F32) / MLA_ROPE))
    pos = jnp.repeat(positions.reshape(t // per_row, per_row), half, axis=1)
    rows = t // per_row
    tile = 512
    cos, sin = pl.pallas_call(
        _rope_table_kernel,
        grid=(rows // tile,),
        in_specs=[_row_spec(tile, LANES), _const_spec((1, LANES))],
        out_specs=(_row_spec(tile, LANES), _row_spec(tile, LANES)),
        out_shape=(jax.ShapeDtypeStruct((rows, LANES), F32),) * 2,
        compiler_params=_params(1),
        name="rope_table",
    )(pos, jnp.tile(inv_freq, per_row).reshape(1, LANES))
    widen = lambda a: jnp.tile(a.reshape(t, half), (1, per_row))
    return widen(cos), widen(sin)


def _mla_in_kernel(x_ref, g_ref, wcq_ref, wckv_ref, wkpe_ref, wkper_ref, gq_ref, gkv_ref,
                   wqn_ref, wqp_ref, wqpr_ref, wkn_ref, wv_ref, cos_ref, sin_ref,
                   q_out, k_out, vt_out):
    h = _rms(x_ref[...], g_ref[...]).astype(BF16)
    cos = cos_ref[...]
    sin = sin_ref[...]
    kpe = (_dot(h, wkpe_ref[...]) * cos + _dot(h, wkper_ref[...]) * sin).astype(BF16)
    hq = _rms(_dot(h, wcq_ref[...]), gq_ref[...]).astype(BF16)
    hkv = _rms(_dot(h, wckv_ref[...]), gkv_ref[...]).astype(BF16)
    scale = MLA_QK ** -0.5 * math.log2(math.e)
    qn = _dot(hq, wqn_ref[...])
    cos8 = jnp.concatenate([cos] * MLA_HEADS, axis=-1)
    sin8 = jnp.concatenate([sin] * MLA_HEADS, axis=-1)
    qp = _dot(hq, wqp_ref[...]) * cos8 + _dot(hq, wqpr_ref[...]) * sin8
    kn = _dot(hkv, wkn_ref[...])
    vt_out[...] = _dot(hkv, wv_ref[...]).T.astype(BF16)
    for hd in range(MLA_HEADS):
        cols = slice(hd * LANES, (hd + 1) * LANES)
        base = hd * MLA_HEAD_SLOT
        q_out[:, base:base + LANES] = (qn[:, cols] * scale).astype(BF16)
        q_out[:, base + LANES:base + 2 * LANES] = (qp[:, cols] * scale).astype(BF16)
        k_out[:, base:base + LANES] = kn[:, cols].astype(BF16)
        k_out[:, base + LANES:base + 2 * LANES] = kpe


def _mla_in(x, g, w, cos, sin):
    t = x.shape[0]
    weights = [w["wcq"], w["wckv"], w["wkpe"], w["wkper"], w["gq"], w["gkv"],
               w["wqn"], w["wqp"], w["wqpr"], w["wkn"], w["wv"]]
    width = MLA_HEADS * MLA_HEAD_SLOT
    out_shape = (jax.ShapeDtypeStruct((t, width), BF16), jax.ShapeDtypeStruct((t, width), BF16),
                 jax.ShapeDtypeStruct((MLA_HEADS * MLA_V, t), BF16))
    return pl.pallas_call(
        _mla_in_kernel,
        grid=(t // TOKEN_TILE,),
        in_specs=[_row_spec(TOKEN_TILE, D_MODEL), _const_spec((1, D_MODEL))]
                 + [_const_spec(a.shape) for a in weights]
                 + [_row_spec(TOKEN_TILE, LANES), _row_spec(TOKEN_TILE, LANES)],
        out_specs=(_row_spec(TOKEN_TILE, width), _row_spec(TOKEN_TILE, width),
                   pl.BlockSpec((MLA_HEADS * MLA_V, TOKEN_TILE), lambda i: (0, i))),
        out_shape=out_shape,
        compiler_params=_params(1),
        name="mla_in",
    )(x, g.reshape(1, D_MODEL), *weights, cos, sin)


def _mla_attn_kernel(q_ref, k_ref, vt_ref, o_ref, s_ref):
    n_blocks = SEQ // ATTN_KEY_BLOCK

    def scores(j):
        keys = slice(j * ATTN_KEY_BLOCK, (j + 1) * ATTN_KEY_BLOCK)
        s_ref[j % 2] = _dot_nt(k_ref[keys, :], q_ref[...])

    scores(0)
    ones = jnp.ones((2 * SUBLANES, ATTN_KEY_BLOCK), BF16)
    m = acc = None
    for j in range(n_blocks):
        keys = slice(j * ATTN_KEY_BLOCK, (j + 1) * ATTN_KEY_BLOCK)
        if j + 1 < n_blocks:
            scores(j + 1)
        s = s_ref[j % 2]
        m_blk = jnp.max(s, axis=0, keepdims=True)
        vt_ones = jnp.concatenate([vt_ref[:, keys], ones], axis=0)
        if j == 0:
            m = m_blk
            acc = _dot(vt_ones, jnp.exp2(s - m).astype(BF16))
        else:
            m_new = jnp.maximum(m, m_blk)
            acc = jnp.exp2(m - m_new) * acc + _dot(vt_ones, jnp.exp2(s - m_new).astype(BF16))
            m = m_new
    o_ref[...] = (acc[:MLA_V] / acc[MLA_V:MLA_V + 1]).T.astype(o_ref.dtype)


def _mla_attn(q, k, vt):
    t = q.shape[0]
    n_q = SEQ // ATTN_Q_TILE
    return pl.pallas_call(
        _mla_attn_kernel,
        grid=(BATCH, MLA_HEADS, n_q),
        in_specs=[pl.BlockSpec((ATTN_Q_TILE, MLA_HEAD_SLOT), lambda b, h, i: (b * n_q + i, h)),
                  pl.BlockSpec((SEQ, MLA_HEAD_SLOT), lambda b, h, i: (b, h)),
                  pl.BlockSpec((MLA_V, SEQ), lambda b, h, i: (h, b))],
        out_specs=pl.BlockSpec((ATTN_Q_TILE, MLA_V), lambda b, h, i: (b * n_q + i, h)),
        out_shape=jax.ShapeDtypeStruct((t, MLA_HEADS * MLA_V), BF16),
        scratch_shapes=[pltpu.VMEM((2, ATTN_KEY_BLOCK, ATTN_Q_TILE), F32)],
        compiler_params=_params(3),
        name="mla_attn",
    )(q, k, vt)


def _proj_residual_kernel(a_ref, x_ref, w_ref, y_ref):
    y_ref[...] = x_ref[...] + _dot(a_ref[...], w_ref[...])


def _mla_out(a, x, w_out):
    t = x.shape[0]
    return pl.pallas_call(
        _proj_residual_kernel,
        grid=(t // TOKEN_TILE,),
        in_specs=[_row_spec(TOKEN_TILE, a.shape[1]), _row_spec(TOKEN_TILE, D_MODEL),
                  _const_spec(w_out.shape)],
        out_specs=_row_spec(TOKEN_TILE, D_MODEL),
        out_shape=jax.ShapeDtypeStruct((t, D_MODEL), F32),
        compiler_params=_params(1),
        name="mla_out",
    )(a, x, w_out)


def _rotate_half_columns(w):
    shape = w.shape
    w = w.reshape(shape[0], -1, 2, MLA_ROPE // 2)
    return jnp.stack([-w[:, :, 1], w[:, :, 0]], axis=2).reshape(shape)


def _pad_rope_columns(w):
    k = w.shape[0]
    w = w.reshape(k, -1, MLA_ROPE)
    return jnp.pad(w, ((0, 0), (0, 0), (0, LANES - MLA_ROPE))).reshape(k, -1)


def _mla_weights(w_in, q_norm, kv_norm, w_uq, w_ukv):
    w_kpe = w_in[:, MLA_Q_RANK + MLA_KV_RANK:]
    w_uq = w_uq.reshape(MLA_Q_RANK, MLA_HEADS, MLA_QK)
    w_uq_rope = w_uq[:, :, MLA_NOPE:].reshape(MLA_Q_RANK, MLA_HEADS * MLA_ROPE)
    w_ukv = w_ukv.reshape(MLA_KV_RANK, MLA_HEADS, MLA_NOPE + MLA_V)
    return {
        "wcq": w_in[:, :MLA_Q_RANK].astype(BF16),
        "wckv": w_in[:, MLA_Q_RANK:MLA_Q_RANK + MLA_KV_RANK].astype(BF16),
        "wkpe": _pad_rope_columns(w_kpe).astype(BF16),
        "wkper": _pad_rope_columns(_rotate_half_columns(w_kpe)).astype(BF16),
        "gq": q_norm.reshape(1, MLA_Q_RANK),
        "gkv": kv_norm.reshape(1, MLA_KV_RANK),
        "wqn": w_uq[:, :, :MLA_NOPE].reshape(MLA_Q_RANK, MLA_HEADS * MLA_NOPE).astype(BF16),
        "wqp": _pad_rope_columns(w_uq_rope).astype(BF16),
        "wqpr": _pad_rope_columns(_rotate_half_columns(w_uq_rope)).astype(BF16),
        "wkn": w_ukv[:, :, :MLA_NOPE].reshape(MLA_KV_RANK, MLA_HEADS * MLA_NOPE).astype(BF16),
        "wv": w_ukv[:, :, MLA_NOPE:].reshape(MLA_KV_RANK, MLA_HEADS * MLA_V).astype(BF16),
    }


def kernel(x, positions, ffn_norm, ffn_w_gu, ffn_w_down, mix_norm, gla_w_in, gla_w_gate2, gla_b_gate,
           gla_head_norm, gla_w_out, mla_w_in, mla_q_norm, mla_kv_norm, mla_w_uq, mla_w_ukv, mla_w_out,
           final_norm):
    assert x.shape == (BATCH, SEQ, D_MODEL)
    t = BATCH * SEQ
    x = x.reshape(t, D_MODEL)
    cos, sin = _rope_tables(positions)
    n_main = 2 * GLA_DK_TOT + 2 * GLA_DV_TOT
    zeros = jnp.zeros((GLA_GATE_RANK, GLA_DK_TOT), F32)
    for i in range(DEPTH):
        x = _ffn(x, ffn_norm[i, 0], ffn_w_gu[i, 0].astype(BF16), ffn_w_down[i, 0].astype(BF16))
        j = i // 2
        if i % 2 == 0:
            w_gate2 = jnp.block([[gla_w_gate2[j, 0], zeros], [zeros, gla_w_gate2[j, 1]]]).astype(BF16)
            q, k, v, r, lf, lb = _gla_in(
                x, mix_norm[i], gla_w_in[j, :, :n_main].astype(BF16), gla_w_in[j, :, n_main:].astype(BF16),
                w_gate2, gla_b_gate[j].reshape(1, 2 * GLA_DK_TOT))
            o = _gla_core(q, k, v, lf, lb)
            x = _gla_out(o, r, x, gla_head_norm[j], gla_w_out[j].astype(BF16))
        else:
            w = _mla_weights(mla_w_in[j], mla_q_norm[j], mla_kv_norm[j], mla_w_uq[j], mla_w_ukv[j])
            qf, kf, vt = _mla_in(x, mix_norm[i], w, cos, sin)
            a = _mla_attn(qf, kf, vt)
            x = _mla_out(a, x, mla_w_out[j].astype(BF16))
        x = _ffn(x, ffn_norm[i, 1], ffn_w_gu[i, 1].astype(BF16), ffn_w_down[i, 1].astype(BF16),
                 g_final=final_norm if i == DEPTH - 1 else None)
    return x.reshape(BATCH, SEQ, D_MODEL)
```

```python
import functools
import math

import numpy as np
import jax
import jax.numpy as jnp
from jax import lax
from jax.experimental import pallas as pl
from jax.experimental.pallas import tpu as pltpu

F32 = jnp.float32
BF16 = jnp.bfloat16

D_MODEL = 1024
BATCH = 8
SEQ = 2048
DEPTH = 4
EPS = 1e-6
D_FF = 2816

GLA_HEADS = 4
GLA_DK_TOT = 512
GLA_DV_TOT = 1024
GLA_DK = 128
GLA_DV = 256
GLA_GATE_RANK = 16
GLA_TAU = 16.0

MLA_HEADS = 8
MLA_NOPE = 128
MLA_ROPE = 64
MLA_V = 128
MLA_Q_RANK = 768
MLA_KV_RANK = 256
MLA_QK = MLA_NOPE + MLA_ROPE
ROPE_THETA = 10000.0

LANES = 128
SUBLANES = 8
VMEM_LIMIT_BYTES = 56 * 1024 * 1024

TOKEN_TILE = 512
FFN_CHUNK = 256
GLA_CHUNK = 64
GLA_LEVELS = int(math.log2(GLA_CHUNK))
GLA_SUPER = 256
ATTN_Q_TILE = 2048
ATTN_KEY_BLOCK = 512
MLA_HEAD_SLOT = 2 * LANES


def _params(n_axes):
    return pltpu.CompilerParams(
        dimension_semantics=("parallel",) * n_axes,
        vmem_limit_bytes=VMEM_LIMIT_BYTES)


def _dot(a, b):
    return jnp.dot(a, b, preferred_element_type=F32)


def _dot_nt(a, b):
    return lax.dot_general(a, b, (((1,), (1,)), ((), ())), preferred_element_type=F32)


def _dot_tn(a, b):
    return lax.dot_general(a, b, (((0,), (0,)), ((), ())), preferred_element_type=F32)


def _rms(x, g):
    return x * lax.rsqrt(jnp.mean(x * x, axis=-1, keepdims=True) + EPS) * g


def _row_spec(tile, width):
    return pl.BlockSpec((tile, width), lambda i: (i, 0))


def _const_spec(shape):
    return pl.BlockSpec(shape, lambda *_: (0,) * len(shape))


def _gla_gated(o_ref, r_ref, g_ref):
    g = g_ref[...]
    parts = []
    for hd in range(GLA_HEADS):
        cols = slice(hd * GLA_DV, (hd + 1) * GLA_DV)
        r = r_ref[:, cols]
        parts.append((_rms(o_ref[:, cols], g) * (r * jax.nn.sigmoid(r))).astype(BF16))
    return jnp.concatenate(parts, axis=-1)


def _ffn_kernel(x_ref, g_ref, wgu_ref, wd_ref, *rest, mixer, final):
    o_ref = rest[-1]
    x = x_ref[...]
    if mixer == "mla":
        a_ref, wo_ref = rest[:2]
        x = x + _dot(a_ref[...], wo_ref[...])
    elif mixer == "gla":
        go_ref, gr_ref, gg_ref, wo_ref = rest[:4]
        x = x + _dot(_gla_gated(go_ref, gr_ref, gg_ref), wo_ref[...])
    h = _rms(x, g_ref[...]).astype(BF16)
    acc = None
    for c in range(D_FF // FFN_CHUNK):
        lo = c * FFN_CHUNK
        gate = _dot(h, wgu_ref[:, lo:lo + FFN_CHUNK])
        up = _dot(h, wgu_ref[:, D_FF + lo:D_FF + lo + FFN_CHUNK])
        act = (gate * jax.nn.sigmoid(gate) * up).astype(BF16)
        part = _dot(act, wd_ref[lo:lo + FFN_CHUNK, :])
        acc = part if acc is None else acc + part
    y = x + 0.5 * acc
    if final:
        y = _rms(y, rest[-2][...])
    o_ref[...] = y


def _resident_spec(shape, index):
    return pl.BlockSpec(shape, lambda *_: index, pipeline_mode=pl.Buffered(1))


def _ffn(x, g, wgu_all, wd_all, layer, half, mixer=None, mixer_args=(), g_final=None):
    t = x.shape[0]
    final = g_final is not None
    pick = (layer, half, 0, 0)
    in_specs = [_row_spec(TOKEN_TILE, D_MODEL), _const_spec((1, D_MODEL)),
                _resident_spec((None, None, D_MODEL, 2 * D_FF), pick),
                _resident_spec((None, None, D_FF, D_MODEL), pick)]
    args = [x, g.reshape(1, D_MODEL), wgu_all, wd_all]
    if mixer == "mla":
        a, w_out = mixer_args
        in_specs += [_row_spec(TOKEN_TILE, a.shape[1]), _resident_spec(w_out.shape, (0, 0))]
        args += [a, w_out]
    elif mixer == "gla":
        o, r, g_head, w_out = mixer_args
        in_specs += [_row_spec(TOKEN_TILE, GLA_DV_TOT), _row_spec(TOKEN_TILE, GLA_DV_TOT),
                     _const_spec((1, GLA_DV)), _resident_spec(w_out.shape, (0, 0))]
        args += [o, r, g_head.reshape(1, GLA_DV), w_out]
    if final:
        in_specs.append(_const_spec((1, D_MODEL)))
        args.append(g_final.reshape(1, D_MODEL))
    return pl.pallas_call(
        functools.partial(_ffn_kernel, mixer=mixer, final=final),
        grid=(t // TOKEN_TILE,),
        in_specs=in_specs,
        out_specs=_row_spec(TOKEN_TILE, D_MODEL),
        out_shape=jax.ShapeDtypeStruct((t, D_MODEL), F32),
        compiler_params=_params(1),
        name="ffn" + ("_" + mixer if mixer else "") + ("_final" if final else ""),
    )(*args)


def _gla_in_kernel(x_ref, g_ref, w_ref, wg_ref, w2_ref, b2_ref,
                   q_ref, k_ref, v_ref, r_ref, lf_ref, lb_ref):
    h = _rms(x_ref[...], g_ref[...]).astype(BF16)
    y = _dot(h, w_ref[...])
    q_ref[...] = (y[:, :GLA_DK_TOT] * (GLA_DK ** -0.5)).astype(BF16)
    k_ref[...] = y[:, GLA_DK_TOT:2 * GLA_DK_TOT].astype(BF16)
    v_ref[...] = y[:, 2 * GLA_DK_TOT:2 * GLA_DK_TOT + GLA_DV_TOT].astype(BF16)
    r_ref[...] = y[:, 2 * GLA_DK_TOT + GLA_DV_TOT:]
    low = _dot(h, wg_ref[...]).astype(BF16)
    z = _dot(low, w2_ref[...]) + b2_ref[...]
    log_a = (jnp.minimum(z, 0.0) - jnp.log1p(jnp.exp(-jnp.abs(z)))) * (math.log2(math.e) / GLA_TAU)
    lf_ref[...] = log_a[:, :GLA_DK_TOT]
    lb_ref[...] = log_a[:, GLA_DK_TOT:]


def _gla_in(x, g, w_main, w_gate1, w_gate2, b_gate):
    t = x.shape[0]
    n_main = 2 * GLA_DK_TOT + 2 * GLA_DV_TOT
    out_shape = (
        jax.ShapeDtypeStruct((t, GLA_DK_TOT), BF16), jax.ShapeDtypeStruct((t, GLA_DK_TOT), BF16),
        jax.ShapeDtypeStruct((t, GLA_DV_TOT), BF16), jax.ShapeDtypeStruct((t, GLA_DV_TOT), F32),
        jax.ShapeDtypeStruct((t, GLA_DK_TOT), F32), jax.ShapeDtypeStruct((t, GLA_DK_TOT), F32))
    return pl.pallas_call(
        _gla_in_kernel,
        grid=(t // TOKEN_TILE,),
        in_specs=[_row_spec(TOKEN_TILE, D_MODEL), _const_spec((1, D_MODEL)),
                  _const_spec((D_MODEL, n_main)), _const_spec((D_MODEL, 2 * GLA_GATE_RANK)),
                  _const_spec((2 * GLA_GATE_RANK, 2 * GLA_DK_TOT)), _const_spec((1, 2 * GLA_DK_TOT))],
        out_specs=tuple(_row_spec(TOKEN_TILE, s.shape[1]) for s in out_shape),
        out_shape=out_shape,
        compiler_params=_params(1),
        name="gla_in",
    )(x, g.reshape(1, D_MODEL), w_main, w_gate1, w_gate2, b_gate)


def _gla_constants():
    ch, nl, sup = GLA_CHUNK, GLA_LEVELS, GLA_SUPER
    tri = np.tril(np.ones((ch, ch), np.float32))
    eye = np.eye(sup // ch, dtype=np.float32)
    cum_fw = np.kron(eye, tri)
    cum_bw = np.kron(eye, tri.T)
    mask = np.zeros((2 * nl + 1, ch, ch), np.float32)
    t = np.arange(ch)[:, None]
    s = np.arange(ch)[None, :]
    for lvl in range(nl):
        w = 1 << lvl
        sel = (t // (2 * w) == s // (2 * w)) & (t % (2 * w) >= w) & (s % (2 * w) < w)
        mask[lvl] = sel
        mask[nl + 1 + lvl] = sel.T
    mask[nl] = np.eye(ch, dtype=np.float32)
    return jnp.asarray(cum_fw, BF16), jnp.asarray(cum_bw, BF16), jnp.asarray(mask, F32)


def _level_reference(b, lvl):
    ch = GLA_CHUNK
    w = 1 << lvl
    if 2 * w >= 2 * SUBLANES:
        parts = [jnp.broadcast_to(b[blk + w:blk + w + 1, :], (2 * w, LANES))
                 for blk in range(0, ch, 2 * w)]
        return parts[0] if len(parts) == 1 else jnp.concatenate(parts, axis=0)
    b3 = b.reshape(ch // SUBLANES, SUBLANES, LANES)
    sub = lax.broadcasted_iota(jnp.int32, b3.shape, 1)
    ref = None
    for blk in range(SUBLANES - 2 * w, -1, -2 * w):
        row = jnp.broadcast_to(b3[:, blk + w:blk + w + 1, :], b3.shape)
        ref = row if ref is None else jnp.where(sub < blk + 2 * w, row, ref)
    return ref.reshape(ch, LANES)


def _gla_core_kernel(q_ref, k_ref, v_ref, lf_ref, lb_ref, cf_ref, cb_ref, mask_ref,
                     o_ref, sf_ref, sb_ref):
    ch, nl, sup = GLA_CHUNK, GLA_LEVELS, GLA_SUPER
    n_super = SEQ // sup
    sf_ref[...] = jnp.zeros_like(sf_ref)
    sb_ref[...] = jnp.zeros_like(sb_ref)
    o_ref[...] = jnp.zeros_like(o_ref)

    def cumulative(row0, la_ref, cum_ref):
        la = la_ref[pl.ds(row0, sup), :]
        la_hi = la.astype(BF16)
        la_lo = (la - la_hi.astype(F32)).astype(BF16)
        return _dot(cum_ref[...], la_hi) + _dot(cum_ref[...], la_lo)

    def intra(row0, j, cum, forward):
        rows = pl.ds(row0 + j * ch, ch)
        q = q_ref[rows, :]
        k = k_ref[rows, :]
        v = v_ref[rows, :]
        b = cum[j * ch:(j + 1) * ch]
        p = None
        for lvl in range(nl):
            x = jnp.exp2(-jnp.abs(b - _level_reference(b, lvl))).astype(BF16)
            sc = _dot_nt(q * x, k * x) * mask_ref[lvl if forward else nl + 1 + lvl]
            p = sc if p is None else p + sc
        if forward:
            p = p + _dot_nt(q, k) * mask_ref[nl]
        edge = b[ch - 1:ch, :] if forward else b[0:1, :]
        qd = q * jnp.exp2(b).astype(BF16)
        kd = k * jnp.exp2(edge - b).astype(BF16)
        decay = jnp.broadcast_to(jnp.exp2(edge), (GLA_DK, GLA_DK)).T
        return rows, jnp.concatenate([qd, p.astype(BF16)], axis=1), v, decay, _dot_tn(kd, v)

    def inter(st_ref, rows, qd_p, v, decay, update):
        st = st_ref[...]
        o_ref[rows, :] += _dot(qd_p, jnp.concatenate([st.astype(BF16), v], axis=0))
        st_ref[...] = st * jnp.concatenate([decay, decay], axis=1) + update

    def body(i, carry):
        n = sup // ch
        row_f = pl.multiple_of(i * sup, sup)
        row_b = pl.multiple_of((n_super - 1 - i) * sup, sup)
        cum_f = cumulative(row_f, lf_ref, cf_ref)
        cum_b = cumulative(row_b, lb_ref, cb_ref)
        done = []
        for j in range(n):
            done.append((sf_ref, intra(row_f, j, cum_f, True)))
            done.append((sb_ref, intra(row_b, n - 1 - j, cum_b, False)))
        for st_ref, parts in done:
            inter(st_ref, *parts)
        return carry

    lax.fori_loop(0, n_super, body, 0)


def _gla_core(q, k, v, lf, lb):
    t = q.shape[0]
    cum_fw, cum_bw, mask = _gla_constants()
    seq_blk = lambda width: pl.BlockSpec((SEQ, width), lambda b, h: (b, h))
    return pl.pallas_call(
        _gla_core_kernel,
        grid=(BATCH, GLA_HEADS),
        in_specs=[seq_blk(GLA_DK), seq_blk(GLA_DK), seq_blk(GLA_DV), seq_blk(GLA_DK), seq_blk(GLA_DK),
                  _const_spec(cum_fw.shape), _const_spec(cum_bw.shape), _const_spec(mask.shape)],
        out_specs=seq_blk(GLA_DV),
        out_shape=jax.ShapeDtypeStruct((t, GLA_DV_TOT), F32),
        scratch_shapes=[pltpu.VMEM((GLA_DK, GLA_DV), F32), pltpu.VMEM((GLA_DK, GLA_DV), F32)],
        compiler_params=_params(2),
        name="gla_core",
    )(q, k, v, lf, lb, cum_fw, cum_bw, mask)


def _rope_table_kernel(pos_ref, freq_ref, cos_ref, sin_ref):
    ang = pos_ref[...].astype(F32) * freq_ref[...]
    cos_ref[...] = jnp.cos(ang)
    sin_ref[...] = jnp.sin(ang)


def _rope_tables(positions):
    half = MLA_ROPE // 2
    per_row = LANES // half
    t = positions.size
    inv_freq = 1.0 / (ROPE_THETA ** (jnp.arange(0, MLA_ROPE, 2, dtype=F32) / MLA_ROPE))
    pos = jnp.repeat(positions.reshape(t // per_row, per_row), half, axis=1)
    rows = t // per_row
    tile = 512
    cos, sin = pl.pallas_call(
        _rope_table_kernel,
        grid=(rows // tile,),
        in_specs=[_row_spec(tile, LANES), _const_spec((1, LANES))],
        out_specs=(_row_spec(tile, LANES), _row_spec(tile, LANES)),
        out_shape=(jax.ShapeDtypeStruct((rows, LANES), F32),) * 2,
        compiler_params=_params(1),
        name="rope_table",
    )(pos, jnp.tile(inv_freq, per_row).reshape(1, LANES))
    widen = lambda a: jnp.tile(a.reshape(t, half), (1, per_row))
    return widen(cos), widen(sin)


def _mla_in_kernel(x_ref, g_ref, wcq_ref, wckv_ref, wkpe_ref, gq_ref, gkv_ref,
                   wqn_ref, wqp_ref, wqpr_ref, wkn_ref, wv_ref, cos_ref, sin_ref,
                   q_out, k_out, vt_out):
    h = _rms(x_ref[...], g_ref[...]).astype(BF16)
    cos = cos_ref[...]
    sin = sin_ref[...]
    kpe2 = _dot(h, wkpe_ref[...])
    kpe = (kpe2[:, :LANES] * cos + kpe2[:, LANES:] * sin).astype(BF16)
    hq = _rms(_dot(h, wcq_ref[...]), gq_ref[...]).astype(BF16)
    hkv = _rms(_dot(h, wckv_ref[...]), gkv_ref[...]).astype(BF16)
    scale = MLA_QK ** -0.5 * math.log2(math.e)
    qn = _dot(hq, wqn_ref[...])
    pairs = MLA_HEADS // 2
    cos4 = jnp.concatenate([cos] * pairs, axis=-1)
    sin4 = jnp.concatenate([sin] * pairs, axis=-1)
    qp = (_dot(hq, wqp_ref[...]) * cos4 + _dot(hq, wqpr_ref[...]) * sin4) * scale
    kn = _dot(hkv, wkn_ref[...])
    vt_out[...] = _dot(hkv, wv_ref[...]).T.astype(BF16)
    low_half = lax.broadcasted_iota(jnp.int32, (qp.shape[0], LANES), 1) < MLA_ROPE
    for hd in range(MLA_HEADS):
        cols = slice(hd * LANES, (hd + 1) * LANES)
        base = hd * MLA_HEAD_SLOT
        pair = qp[:, (hd // 2) * LANES:(hd // 2 + 1) * LANES]
        if hd % 2:
            pair = pltpu.roll(pair, MLA_ROPE, axis=1)
        q_out[:, base:base + LANES] = (qn[:, cols] * scale).astype(BF16)
        q_out[:, base + LANES:base + 2 * LANES] = jnp.where(low_half, pair, 0.0).astype(BF16)
        k_out[:, base:base + LANES] = kn[:, cols].astype(BF16)
        k_out[:, base + LANES:base + 2 * LANES] = kpe


def _mla_in(x, g, w, cos, sin):
    t = x.shape[0]
    weights = [w["wcq"], w["wckv"], w["wkpe"], w["gq"], w["gkv"],
               w["wqn"], w["wqp"], w["wqpr"], w["wkn"], w["wv"]]
    width = MLA_HEADS * MLA_HEAD_SLOT
    out_shape = (jax.ShapeDtypeStruct((t, width), BF16), jax.ShapeDtypeStruct((t, width), BF16),
                 jax.ShapeDtypeStruct((MLA_HEADS * MLA_V, t), BF16))
    return pl.pallas_call(
        _mla_in_kernel,
        grid=(t // TOKEN_TILE,),
        in_specs=[_row_spec(TOKEN_TILE, D_MODEL), _const_spec((1, D_MODEL))]
                 + [_const_spec(a.shape) for a in weights]
                 + [_row_spec(TOKEN_TILE, LANES), _row_spec(TOKEN_TILE, LANES)],
        out_specs=(_row_spec(TOKEN_TILE, width), _row_spec(TOKEN_TILE, width),
                   pl.BlockSpec((MLA_HEADS * MLA_V, TOKEN_TILE), lambda i: (0, i))),
        out_shape=out_shape,
        compiler_params=_params(1),
        name="mla_in",
    )(x, g.reshape(1, D_MODEL), *weights, cos, sin)


def _mla_attn_kernel(q_ref, k_ref, vt_ref, o_ref, s_ref):
    n_blocks = SEQ // ATTN_KEY_BLOCK

    def scores(j):
        keys = slice(j * ATTN_KEY_BLOCK, (j + 1) * ATTN_KEY_BLOCK)
        s_ref[j % 2] = _dot_nt(k_ref[keys, :], q_ref[...])

    scores(0)
    ones = jnp.ones((2 * SUBLANES, ATTN_KEY_BLOCK), BF16)
    m = acc = None
    for j in range(n_blocks):
        keys = slice(j * ATTN_KEY_BLOCK, (j + 1) * ATTN_KEY_BLOCK)
        if j + 1 < n_blocks:
            scores(j + 1)
        s = s_ref[j % 2]
        m_blk = jnp.max(s, axis=0, keepdims=True)
        vt_ones = jnp.concatenate([vt_ref[:, keys], ones], axis=0)
        if j == 0:
            m = m_blk
            acc = _dot(vt_ones, jnp.exp2(s - m).astype(BF16))
        else:
            m_new = jnp.maximum(m, m_blk)
            acc = jnp.exp2(m - m_new) * acc + _dot(vt_ones, jnp.exp2(s - m_new).astype(BF16))
            m = m_new
    o_ref[...] = (acc[:MLA_V] / acc[MLA_V:MLA_V + 1]).T.astype(o_ref.dtype)


def _mla_attn(q, k, vt):
    t = q.shape[0]
    n_q = SEQ // ATTN_Q_TILE
    return pl.pallas_call(
        _mla_attn_kernel,
        grid=(BATCH, MLA_HEADS, n_q),
        in_specs=[pl.BlockSpec((ATTN_Q_TILE, MLA_HEAD_SLOT), lambda b, h, i: (b * n_q + i, h)),
                  pl.BlockSpec((SEQ, MLA_HEAD_SLOT), lambda b, h, i: (b, h)),
                  pl.BlockSpec((MLA_V, SEQ), lambda b, h, i: (h, b))],
        out_specs=pl.BlockSpec((ATTN_Q_TILE, MLA_V), lambda b, h, i: (b * n_q + i, h)),
        out_shape=jax.ShapeDtypeStruct((t, MLA_HEADS * MLA_V), BF16),
        scratch_shapes=[pltpu.VMEM((2, ATTN_KEY_BLOCK, ATTN_Q_TILE), F32)],
        compiler_params=_params(3),
        name="mla_attn",
    )(q, k, vt)


def _rotate_half_columns(w):
    shape = w.shape
    w = w.reshape(shape[0], -1, 2, MLA_ROPE // 2)
    return jnp.stack([-w[:, :, 1], w[:, :, 0]], axis=2).reshape(shape)


def _pad_rope_columns(w):
    k = w.shape[0]
    w = w.reshape(k, -1, MLA_ROPE)
    return jnp.pad(w, ((0, 0), (0, 0), (0, LANES - MLA_ROPE))).reshape(k, -1)


def _mla_weights(w_in, q_norm, kv_norm, w_uq, w_ukv):
    w_kpe = w_in[:, MLA_Q_RANK + MLA_KV_RANK:]
    w_uq = w_uq.reshape(MLA_Q_RANK, MLA_HEADS, MLA_QK)
    w_uq_rope = w_uq[:, :, MLA_NOPE:].reshape(MLA_Q_RANK, MLA_HEADS * MLA_ROPE)
    w_ukv = w_ukv.reshape(MLA_KV_RANK, MLA_HEADS, MLA_NOPE + MLA_V)
    return {
        "wcq": w_in[:, :MLA_Q_RANK].astype(BF16),
        "wckv": w_in[:, MLA_Q_RANK:MLA_Q_RANK + MLA_KV_RANK].astype(BF16),
        "wkpe": jnp.concatenate([_pad_rope_columns(w_kpe),
                                 _pad_rope_columns(_rotate_half_columns(w_kpe))], axis=1).astype(BF16),
        "gq": q_norm.reshape(1, MLA_Q_RANK),
        "gkv": kv_norm.reshape(1, MLA_KV_RANK),
        "wqn": w_uq[:, :, :MLA_NOPE].reshape(MLA_Q_RANK, MLA_HEADS * MLA_NOPE).astype(BF16),
        "wqp": w_uq_rope.astype(BF16),
        "wqpr": _rotate_half_columns(w_uq_rope).astype(BF16),
        "wkn": w_ukv[:, :, :MLA_NOPE].reshape(MLA_KV_RANK, MLA_HEADS * MLA_NOPE).astype(BF16),
        "wv": w_ukv[:, :, MLA_NOPE:].reshape(MLA_KV_RANK, MLA_HEADS * MLA_V).astype(BF16),
    }


def kernel(x, positions, ffn_norm, ffn_w_gu, ffn_w_down, mix_norm, gla_w_in, gla_w_gate2, gla_b_gate,
           gla_head_norm, gla_w_out, mla_w_in, mla_q_norm, mla_kv_norm, mla_w_uq, mla_w_ukv, mla_w_out,
           final_norm):
    assert x.shape == (BATCH, SEQ, D_MODEL)
    t = BATCH * SEQ
    x = x.reshape(t, D_MODEL)
    cos, sin = _rope_tables(positions)
    n_main = 2 * GLA_DK_TOT + 2 * GLA_DV_TOT
    zeros = jnp.zeros((GLA_GATE_RANK, GLA_DK_TOT), F32)
    wgu_all = ffn_w_gu.astype(BF16)
    wd_all = ffn_w_down.astype(BF16)
    for i in range(DEPTH):
        x = _ffn(x, ffn_norm[i, 0], wgu_all, wd_all, i, 0)
        j = i // 2
        if i % 2 == 0:
            w_gate2 = jnp.block([[gla_w_gate2[j, 0], zeros], [zeros, gla_w_gate2[j, 1]]]).astype(BF16)
            q, k, v, r, lf, lb = _gla_in(
                x, mix_norm[i], gla_w_in[j, :, :n_main].astype(BF16), gla_w_in[j, :, n_main:].astype(BF16),
                w_gate2, gla_b_gate[j].reshape(1, 2 * GLA_DK_TOT))
            o = _gla_core(q, k, v, lf, lb)
            mixer, mixer_args = "gla", (o, r, gla_head_norm[j], gla_w_out[j].astype(BF16))
        else:
            w = _mla_weights(mla_w_in[j], mla_q_norm[j], mla_kv_norm[j], mla_w_uq[j], mla_w_ukv[j])
            qf, kf, vt = _mla_in(x, mix_norm[i], w, cos, sin)
            mixer, mixer_args = "mla", (_mla_attn(qf, kf, vt), mla_w_out[j].astype(BF16))
        x = _ffn(x, ffn_norm[i, 1], wgu_all, wd_all, i, 1, mixer=mixer, mixer_args=mixer_args,
                 g_final=final_norm if i == DEPTH - 1 else None)
    return x.reshape(BATCH, SEQ, D_MODEL)
```

```python
import functools
import math

import numpy as np
import jax
import jax.numpy as jnp
from jax import lax
from jax.experimental import pallas as pl
from jax.experimental.pallas import tpu as pltpu

F32 = jnp.float32
BF16 = jnp.bfloat16

D_MODEL = 1024
BATCH = 8
SEQ = 2048
DEPTH = 4
EPS = 1e-6
D_FF = 2816

GLA_HEADS = 4
GLA_DK_TOT = 512
GLA_DV_TOT = 1024
GLA_DK = 128
GLA_DV = 256
GLA_GATE_RANK = 16
GLA_TAU = 16.0

MLA_HEADS = 8
MLA_NOPE = 128
MLA_ROPE = 64
MLA_V = 128
MLA_Q_RANK = 768
MLA_KV_RANK = 256
MLA_QK = MLA_NOPE + MLA_ROPE
ROPE_THETA = 10000.0

LANES = 128
SUBLANES = 8
VMEM_CAPACITY_BYTES = 64 * 1024 * 1024
VMEM_LIMIT_BYTES = VMEM_CAPACITY_BYTES - 6 * 1024 * 1024

TOKEN_TILE = 512
FFN_CHUNK = 256
GLA_CHUNK = 64
GLA_LEVELS = int(math.log2(GLA_CHUNK))
GLA_SUPER = 256
ATTN_Q_TILE = 2048
ATTN_KEY_BLOCK = 512
MLA_HEAD_SLOT = 2 * LANES


def _params(n_axes):
    return pltpu.CompilerParams(
        dimension_semantics=("parallel",) * n_axes,
        vmem_limit_bytes=VMEM_LIMIT_BYTES)


def _dot(a, b):
    return jnp.dot(a, b, preferred_element_type=F32)


def _dot_nt(a, b):
    return lax.dot_general(a, b, (((1,), (1,)), ((), ())), preferred_element_type=F32)


def _dot_tn(a, b):
    return lax.dot_general(a, b, (((0,), (0,)), ((), ())), preferred_element_type=F32)


def _rms(x, g):
    return x * lax.rsqrt(jnp.mean(x * x, axis=-1, keepdims=True) + EPS) * g


def _row_spec(tile, width):
    return pl.BlockSpec((tile, width), lambda i: (i, 0))


def _const_spec(shape):
    return pl.BlockSpec(shape, lambda *_: (0,) * len(shape))


def _gla_gated(o_ref, r_ref, g_ref):
    g = g_ref[...]
    parts = []
    for hd in range(GLA_HEADS):
        cols = slice(hd * GLA_DV, (hd + 1) * GLA_DV)
        r = r_ref[:, cols]
        parts.append((_rms(o_ref[:, cols], g) * (r * jax.nn.sigmoid(r))).astype(BF16))
    return jnp.concatenate(parts, axis=-1)


def _ffn_kernel(x_ref, g_ref, wgu_ref, wd_ref, *rest, mixer, final):
    o_ref = rest[-1]
    x = x_ref[...]
    if mixer == "mla":
        a_ref, wo_ref = rest[:2]
        x = x + _dot(a_ref[...], wo_ref[...])
    elif mixer == "gla":
        go_ref, gr_ref, gg_ref, wo_ref = rest[:4]
        x = x + _dot(_gla_gated(go_ref, gr_ref, gg_ref), wo_ref[...])
    h = _rms(x, g_ref[...]).astype(BF16)
    acc = None
    for c in range(D_FF // FFN_CHUNK):
        lo = c * FFN_CHUNK
        gate = _dot(h, wgu_ref[:, lo:lo + FFN_CHUNK].astype(BF16))
        up = _dot(h, wgu_ref[:, D_FF + lo:D_FF + lo + FFN_CHUNK].astype(BF16))
        act = (gate * jax.nn.sigmoid(gate) * up).astype(BF16)
        part = _dot(act, wd_ref[lo:lo + FFN_CHUNK, :].astype(BF16))
        acc = part if acc is None else acc + part
    y = x + 0.5 * acc
    if final:
        y = _rms(y, rest[-2][...])
    o_ref[...] = y


def _resident_spec(shape, index):
    return pl.BlockSpec(shape, lambda *_: index, pipeline_mode=pl.Buffered(1))


def _ffn(x, g, wgu_all, wd_all, layer, half, mixer=None, mixer_args=(), g_final=None):
    t = x.shape[0]
    final = g_final is not None
    pick = (layer, half, 0, 0)
    in_specs = [_row_spec(TOKEN_TILE, D_MODEL), _const_spec((1, D_MODEL)),
                _resident_spec((None, None, D_MODEL, 2 * D_FF), pick),
                _resident_spec((None, None, D_FF, D_MODEL), pick)]
    args = [x, g.reshape(1, D_MODEL), wgu_all, wd_all]
    if mixer == "mla":
        a, w_out = mixer_args
        in_specs += [_row_spec(TOKEN_TILE, a.shape[1]), _resident_spec(w_out.shape, (0, 0))]
        args += [a, w_out]
    elif mixer == "gla":
        o, r, g_head, w_out = mixer_args
        in_specs += [_row_spec(TOKEN_TILE, GLA_DV_TOT), _row_spec(TOKEN_TILE, GLA_DV_TOT),
                     _const_spec((1, GLA_DV)), _resident_spec(w_out.shape, (0, 0))]
        args += [o, r, g_head.reshape(1, GLA_DV), w_out]
    if final:
        in_specs.append(_const_spec((1, D_MODEL)))
        args.append(g_final.reshape(1, D_MODEL))
    return pl.pallas_call(
        functools.partial(_ffn_kernel, mixer=mixer, final=final),
        grid=(t // TOKEN_TILE,),
        in_specs=in_specs,
        out_specs=_row_spec(TOKEN_TILE, D_MODEL),
        out_shape=jax.ShapeDtypeStruct((t, D_MODEL), F32),
        compiler_params=_params(1),
        name="ffn" + ("_" + mixer if mixer else "") + ("_final" if final else ""),
    )(*args)


def _gla_in_kernel(x_ref, g_ref, w_ref, wg_ref, w2_ref, b2_ref,
                   q_ref, k_ref, v_ref, r_ref, lf_ref, lb_ref):
    h = _rms(x_ref[...], g_ref[...]).astype(BF16)
    y = _dot(h, w_ref[...])
    q_ref[...] = y[:, :GLA_DK_TOT] * (GLA_DK ** -0.5)
    k_ref[...] = y[:, GLA_DK_TOT:2 * GLA_DK_TOT]
    v_ref[...] = y[:, 2 * GLA_DK_TOT:2 * GLA_DK_TOT + GLA_DV_TOT].astype(BF16)
    r_ref[...] = y[:, 2 * GLA_DK_TOT + GLA_DV_TOT:]
    low = _dot(h, wg_ref[...]).astype(BF16)
    z = _dot(low, w2_ref[...]) + b2_ref[...]
    log_a = (jnp.minimum(z, 0.0) - jnp.log1p(jnp.exp(-jnp.abs(z)))) * (math.log2(math.e) / GLA_TAU)
    lf_ref[...] = log_a[:, :GLA_DK_TOT]
    lb_ref[...] = log_a[:, GLA_DK_TOT:]


def _gla_in(x, g, w_main, w_gate1, w_gate2, b_gate):
    t = x.shape[0]
    n_main = 2 * GLA_DK_TOT + 2 * GLA_DV_TOT
    out_shape = (
        jax.ShapeDtypeStruct((t, GLA_DK_TOT), F32), jax.ShapeDtypeStruct((t, GLA_DK_TOT), F32),
        jax.ShapeDtypeStruct((t, GLA_DV_TOT), BF16), jax.ShapeDtypeStruct((t, GLA_DV_TOT), F32),
        jax.ShapeDtypeStruct((t, GLA_DK_TOT), F32), jax.ShapeDtypeStruct((t, GLA_DK_TOT), F32))
    return pl.pallas_call(
        _gla_in_kernel,
        grid=(t // TOKEN_TILE,),
        in_specs=[_row_spec(TOKEN_TILE, D_MODEL), _const_spec((1, D_MODEL)),
                  _const_spec((D_MODEL, n_main)), _const_spec((D_MODEL, 2 * GLA_GATE_RANK)),
                  _const_spec((2 * GLA_GATE_RANK, 2 * GLA_DK_TOT)), _const_spec((1, 2 * GLA_DK_TOT))],
        out_specs=tuple(_row_spec(TOKEN_TILE, s.shape[1]) for s in out_shape),
        out_shape=out_shape,
        compiler_params=_params(1),
        name="gla_in",
    )(x, g.reshape(1, D_MODEL), w_main, w_gate1, w_gate2, b_gate)


def _gla_constants():
    ch, nl, sup = GLA_CHUNK, GLA_LEVELS, GLA_SUPER
    tri = np.tril(np.ones((ch, ch), np.float32))
    eye = np.eye(sup // ch, dtype=np.float32)
    cum_fw = np.kron(eye, tri)
    cum_bw = np.kron(eye, tri.T)
    mask = np.zeros((2 * nl + 1, ch, ch), np.float32)
    t = np.arange(ch)[:, None]
    s = np.arange(ch)[None, :]
    for lvl in range(nl):
        w = 1 << lvl
        sel = (t // (2 * w) == s // (2 * w)) & (t % (2 * w) >= w) & (s % (2 * w) < w)
        mask[lvl] = sel
        mask[nl + 1 + lvl] = sel.T
    mask[nl] = np.eye(ch, dtype=np.float32)
    rows = np.arange(ch)
    sign = np.stack([np.where(rows % (2 << lvl) >= (1 << lvl), 1.0, -1.0) for lvl in range(nl)])
    sign[0, 1::2] = 0.0
    sign = np.broadcast_to(sign[:, :, None], (nl, ch, LANES)).astype(np.float32)
    sign = np.concatenate([sign, -sign], axis=0)
    return (jnp.asarray(cum_fw, BF16), jnp.asarray(cum_bw, BF16), jnp.asarray(mask, F32),
            jnp.asarray(sign, F32))


def _level_reference(b, lvl):
    ch = GLA_CHUNK
    w = 1 << lvl
    if lvl == 0:
        return pltpu.roll(b, ch - 1, axis=0)
    if 2 * w >= 2 * SUBLANES:
        parts = [jnp.broadcast_to(b[blk + w:blk + w + 1, :], (2 * w, LANES))
                 for blk in range(0, ch, 2 * w)]
        return parts[0] if len(parts) == 1 else jnp.concatenate(parts, axis=0)
    b3 = b.reshape(ch // SUBLANES, SUBLANES, LANES)
    sub = lax.broadcasted_iota(jnp.int32, b3.shape, 1)
    ref = None
    for blk in range(SUBLANES - 2 * w, -1, -2 * w):
        row = jnp.broadcast_to(b3[:, blk + w:blk + w + 1, :], b3.shape)
        ref = row if ref is None else jnp.where(sub < blk + 2 * w, row, ref)
    return ref.reshape(ch, LANES)


def _gla_core_kernel(q_ref, k_ref, v_ref, lf_ref, lb_ref, cf_ref, cb_ref, mask_ref, sign_ref,
                     o_ref, sf_ref, sb_ref):
    ch, nl, sup = GLA_CHUNK, GLA_LEVELS, GLA_SUPER
    n_super = SEQ // sup
    sf_ref[...] = jnp.zeros_like(sf_ref)
    sb_ref[...] = jnp.zeros_like(sb_ref)
    o_ref[...] = jnp.zeros_like(o_ref)

    def cumulative(row0, la_ref, cum_ref):
        la = la_ref[pl.ds(row0, sup), :]
        la_hi = la.astype(BF16)
        la_lo = (la - la_hi.astype(F32)).astype(BF16)
        return _dot(cum_ref[...], la_hi) + _dot(cum_ref[...], la_lo)

    def intra(row0, j, cum, forward):
        rows = pl.ds(row0 + j * ch, ch)
        q = q_ref[rows, :]
        k = k_ref[rows, :]
        v = v_ref[rows, :]
        b = cum[j * ch:(j + 1) * ch]
        edge = b[ch - 1:ch, :] if forward else b[0:1, :]
        p = None
        for lvl in range(nl):
            sign = sign_ref[lvl if forward else nl + lvl]
            x = jnp.exp2((b - _level_reference(b, lvl)) * sign)
            sc = _dot_nt((q * x).astype(BF16), (k * x).astype(BF16))
            sc = sc * mask_ref[lvl if forward else nl + 1 + lvl]
            p = sc if p is None else p + sc
        if forward:
            p = p + _dot_nt(q.astype(BF16), k.astype(BF16)) * mask_ref[nl]
        qd = (q * jnp.exp2(b)).astype(BF16)
        kd = (k * jnp.exp2(edge - b)).astype(BF16)
        decay = jnp.broadcast_to(jnp.exp2(edge), (GLA_DK, GLA_DK)).T
        return rows, jnp.concatenate([qd, p.astype(BF16)], axis=1), v, decay, _dot_tn(kd, v)

    def inter(st_ref, rows, qd_p, v, decay, update):
        st = st_ref[...]
        o_ref[rows, :] += _dot(qd_p, jnp.concatenate([st.astype(BF16), v], axis=0))
        st_ref[...] = st * jnp.concatenate([decay, decay], axis=1) + update

    def body(i, carry):
        n = sup // ch
        row_f = pl.multiple_of(i * sup, sup)
        row_b = pl.multiple_of((n_super - 1 - i) * sup, sup)
        cum_f = cumulative(row_f, lf_ref, cf_ref)
        cum_b = cumulative(row_b, lb_ref, cb_ref)
        done = []
        for j in range(n):
            done.append((sf_ref, intra(row_f, j, cum_f, True)))
            done.append((sb_ref, intra(row_b, n - 1 - j, cum_b, False)))
        for st_ref, parts in done:
            inter(st_ref, *parts)
        return carry

    lax.fori_loop(0, n_super, body, 0)


def _gla_core(q, k, v, lf, lb):
    t = q.shape[0]
    cum_fw, cum_bw, mask, sign = _gla_constants()
    seq_blk = lambda width: pl.BlockSpec((SEQ, width), lambda b, h: (b, h))
    return pl.pallas_call(
        _gla_core_kernel,
        grid=(BATCH, GLA_HEADS),
        in_specs=[seq_blk(GLA_DK), seq_blk(GLA_DK), seq_blk(GLA_DV), seq_blk(GLA_DK), seq_blk(GLA_DK),
                  _const_spec(cum_fw.shape), _const_spec(cum_bw.shape), _const_spec(mask.shape),
                  _const_spec(sign.shape)],
        out_specs=seq_blk(GLA_DV),
        out_shape=jax.ShapeDtypeStruct((t, GLA_DV_TOT), F32),
        scratch_shapes=[pltpu.VMEM((GLA_DK, GLA_DV), F32), pltpu.VMEM((GLA_DK, GLA_DV), F32)],
        compiler_params=_params(2),
        name="gla_core",
    )(q, k, v, lf, lb, cum_fw, cum_bw, mask, sign)


def _rope_table_kernel(pos_ref, freq_ref, cos_ref, sin_ref):
    ang = pos_ref[...].astype(F32) * freq_ref[...]
    cos_ref[...] = jnp.cos(ang)
    sin_ref[...] = jnp.sin(ang)


def _rope_tables(positions):
    half = MLA_ROPE // 2
    per_row = LANES // half
    t = positions.size
    inv_freq = 1.0 / (ROPE_THETA ** (jnp.arange(0, MLA_ROPE, 2, dtype=F32) / MLA_ROPE))
    pos = jnp.repeat(positions.reshape(t // per_row, per_row), half, axis=1)
    rows = t // per_row
    tile = 512
    cos, sin = pl.pallas_call(
        _rope_table_kernel,
        grid=(rows // tile,),
        in_specs=[_row_spec(tile, LANES), _const_spec((1, LANES))],
        out_specs=(_row_spec(tile, LANES), _row_spec(tile, LANES)),
        out_shape=(jax.ShapeDtypeStruct((rows, LANES), F32),) * 2,
        compiler_params=_params(1),
        name="rope_table",
    )(pos, jnp.tile(inv_freq, per_row).reshape(1, LANES))
    widen = lambda a: jnp.tile(a.reshape(t, half), (1, per_row))
    return widen(cos), widen(sin)


def _mla_in_kernel(x_ref, g_ref, wcq_ref, wckv_ref, wkpe_ref, gq_ref, gkv_ref,
                   wqn_ref, wqp_ref, wqpr_ref, wkn_ref, wv_ref, cos_ref, sin_ref,
                   q_out, k_out, vt_out):
    h = _rms(x_ref[...], g_ref[...]).astype(BF16)
    cos = cos_ref[...]
    sin = sin_ref[...]
    kpe2 = _dot(h, wkpe_ref[...])
    kpe = (kpe2[:, :LANES] * cos + kpe2[:, LANES:] * sin).astype(BF16)
    hq = _rms(_dot(h, wcq_ref[...]), gq_ref[...]).astype(BF16)
    hkv = _rms(_dot(h, wckv_ref[...]), gkv_ref[...]).astype(BF16)
    scale = MLA_QK ** -0.5 * math.log2(math.e)
    qn = _dot(hq, wqn_ref[...])
    pairs = MLA_HEADS // 2
    cos4 = jnp.concatenate([cos] * pairs, axis=-1)
    sin4 = jnp.concatenate([sin] * pairs, axis=-1)
    qp = (_dot(hq, wqp_ref[...]) * cos4 + _dot(hq, wqpr_ref[...]) * sin4) * scale
    kn = _dot(hkv, wkn_ref[...])
    vt_out[...] = _dot(hkv, wv_ref[...]).T.astype(BF16)
    low_half = lax.broadcasted_iota(jnp.int32, (qp.shape[0], LANES), 1) < MLA_ROPE
    for hd in range(MLA_HEADS):
        cols = slice(hd * LANES, (hd + 1) * LANES)
        base = hd * MLA_HEAD_SLOT
        pair = qp[:, (hd // 2) * LANES:(hd // 2 + 1) * LANES]
        if hd % 2:
            pair = pltpu.roll(pair, MLA_ROPE, axis=1)
        q_out[:, base:base + LANES] = (qn[:, cols] * scale).astype(BF16)
        q_out[:, base + LANES:base + 2 * LANES] = jnp.where(low_half, pair, 0.0).astype(BF16)
        k_out[:, base:base + LANES] = kn[:, cols].astype(BF16)
        k_out[:, base + LANES:base + 2 * LANES] = kpe


def _mla_in(x, g, w, cos, sin):
    t = x.shape[0]
    weights = [w["wcq"], w["wckv"], w["wkpe"], w["gq"], w["gkv"],
               w["wqn"], w["wqp"], w["wqpr"], w["wkn"], w["wv"]]
    width = MLA_HEADS * MLA_HEAD_SLOT
    out_shape = (jax.ShapeDtypeStruct((t, width), BF16), jax.ShapeDtypeStruct((t, width), BF16),
                 jax.ShapeDtypeStruct((MLA_HEADS * MLA_V, t), BF16))
    return pl.pallas_call(
        _mla_in_kernel,
        grid=(t // TOKEN_TILE,),
        in_specs=[_row_spec(TOKEN_TILE, D_MODEL), _const_spec((1, D_MODEL))]
                 + [_const_spec(a.shape) for a in weights]
                 + [_row_spec(TOKEN_TILE, LANES), _row_spec(TOKEN_TILE, LANES)],
        out_specs=(_row_spec(TOKEN_TILE, width), _row_spec(TOKEN_TILE, width),
                   pl.BlockSpec((MLA_HEADS * MLA_V, TOKEN_TILE), lambda i: (0, i))),
        out_shape=out_shape,
        compiler_params=_params(1),
        name="mla_in",
    )(x, g.reshape(1, D_MODEL), *weights, cos, sin)


def _mla_attn_kernel(q_ref, k_ref, vt_ref, o_ref, s_ref):
    n_blocks = SEQ // ATTN_KEY_BLOCK

    def scores(j):
        keys = slice(j * ATTN_KEY_BLOCK, (j + 1) * ATTN_KEY_BLOCK)
        s_ref[j % 2] = _dot_nt(k_ref[keys, :], q_ref[...])

    scores(0)
    ones = jnp.ones((2 * SUBLANES, ATTN_KEY_BLOCK), BF16)
    m = acc = None
    for j in range(n_blocks):
        keys = slice(j * ATTN_KEY_BLOCK, (j + 1) * ATTN_KEY_BLOCK)
        if j + 1 < n_blocks:
            scores(j + 1)
        s = s_ref[j % 2]
        m_blk = jnp.max(s, axis=0, keepdims=True)
        vt_ones = jnp.concatenate([vt_ref[:, keys], ones], axis=0)
        if j == 0:
            m = m_blk
            acc = _dot(vt_ones, jnp.exp2(s - m).astype(BF16))
        else:
            m_new = jnp.maximum(m, m_blk)
            acc = jnp.exp2(m - m_new) * acc + _dot(vt_ones, jnp.exp2(s - m_new).astype(BF16))
            m = m_new
    o_ref[...] = (acc[:MLA_V] / acc[MLA_V:MLA_V + 1]).T.astype(o_ref.dtype)


def _mla_attn(q, k, vt):
    t = q.shape[0]
    n_q = SEQ // ATTN_Q_TILE
    return pl.pallas_call(
        _mla_attn_kernel,
        grid=(BATCH, MLA_HEADS, n_q),
        in_specs=[pl.BlockSpec((ATTN_Q_TILE, MLA_HEAD_SLOT), lambda b, h, i: (b * n_q + i, h)),
                  pl.BlockSpec((SEQ, MLA_HEAD_SLOT), lambda b, h, i: (b, h)),
                  pl.BlockSpec((MLA_V, SEQ), lambda b, h, i: (h, b))],
        out_specs=pl.BlockSpec((ATTN_Q_TILE, MLA_V), lambda b, h, i: (b * n_q + i, h)),
        out_shape=jax.ShapeDtypeStruct((t, MLA_HEADS * MLA_V), BF16),
        scratch_shapes=[pltpu.VMEM((2, ATTN_KEY_BLOCK, ATTN_Q_TILE), F32)],
        compiler_params=_params(3),
        name="mla_attn",
    )(q, k, vt)


def _rotate_half_columns(w):
    shape = w.shape
    w = w.reshape(shape[0], -1, 2, MLA_ROPE // 2)
    return jnp.stack([-w[:, :, 1], w[:, :, 0]], axis=2).reshape(shape)


def _pad_rope_columns(w):
    k = w.shape[0]
    w = w.reshape(k, -1, MLA_ROPE)
    return jnp.pad(w, ((0, 0), (0, 0), (0, LANES - MLA_ROPE))).reshape(k, -1)


def _mla_weights(w_in, q_norm, kv_norm, w_uq, w_ukv):
    w_kpe = w_in[:, MLA_Q_RANK + MLA_KV_RANK:]
    w_uq = w_uq.reshape(MLA_Q_RANK, MLA_HEADS, MLA_QK)
    w_uq_rope = w_uq[:, :, MLA_NOPE:].reshape(MLA_Q_RANK, MLA_HEADS * MLA_ROPE)
    w_ukv = w_ukv.reshape(MLA_KV_RANK, MLA_HEADS, MLA_NOPE + MLA_V)
    return {
        "wcq": w_in[:, :MLA_Q_RANK].astype(BF16),
        "wckv": w_in[:, MLA_Q_RANK:MLA_Q_RANK + MLA_KV_RANK].astype(BF16),
        "wkpe": jnp.concatenate([_pad_rope_columns(w_kpe),
                                 _pad_rope_columns(_rotate_half_columns(w_kpe))], axis=1).astype(BF16),
        "gq": q_norm.reshape(1, MLA_Q_RANK),
        "gkv": kv_norm.reshape(1, MLA_KV_RANK),
        "wqn": w_uq[:, :, :MLA_NOPE].reshape(MLA_Q_RANK, MLA_HEADS * MLA_NOPE).astype(BF16),
        "wqp": w_uq_rope.astype(BF16),
        "wqpr": _rotate_half_columns(w_uq_rope).astype(BF16),
        "wkn": w_ukv[:, :, :MLA_NOPE].reshape(MLA_KV_RANK, MLA_HEADS * MLA_NOPE).astype(BF16),
        "wv": w_ukv[:, :, MLA_NOPE:].reshape(MLA_KV_RANK, MLA_HEADS * MLA_V).astype(BF16),
    }


def kernel(x, positions, ffn_norm, ffn_w_gu, ffn_w_down, mix_norm, gla_w_in, gla_w_gate2, gla_b_gate,
           gla_head_norm, gla_w_out, mla_w_in, mla_q_norm, mla_kv_norm, mla_w_uq, mla_w_ukv, mla_w_out,
           final_norm):
    assert x.shape == (BATCH, SEQ, D_MODEL)
    t = BATCH * SEQ
    x = x.reshape(t, D_MODEL)
    cos, sin = _rope_tables(positions)
    n_main = 2 * GLA_DK_TOT + 2 * GLA_DV_TOT
    zeros = jnp.zeros((GLA_GATE_RANK, GLA_DK_TOT), F32)
    wgu_all = ffn_w_gu
    wd_all = ffn_w_down
    for i in range(DEPTH):
        x = _ffn(x, ffn_norm[i, 0], wgu_all, wd_all, i, 0)
        j = i // 2
        if i % 2 == 0:
            w_gate2 = jnp.block([[gla_w_gate2[j, 0], zeros], [zeros, gla_w_gate2[j, 1]]]).astype(BF16)
            q, k, v, r, lf, lb = _gla_in(
                x, mix_norm[i], gla_w_in[j, :, :n_main].astype(BF16), gla_w_in[j, :, n_main:].astype(BF16),
                w_gate2, gla_b_gate[j].reshape(1, 2 * GLA_DK_TOT))
            o = _gla_core(q, k, v, lf, lb)
            mixer, mixer_args = "gla", (o, r, gla_head_norm[j], gla_w_out[j].astype(BF16))
        else:
            w = _mla_weights(mla_w_in[j], mla_q_norm[j], mla_kv_norm[j], mla_w_uq[j], mla_w_ukv[j])
            qf, kf, vt = _mla_in(x, mix_norm[i], w, cos, sin)
            mixer, mixer_args = "mla", (_mla_attn(qf, kf, vt), mla_w_out[j].astype(BF16))
        x = _ffn(x, ffn_norm[i, 1], wgu_all, wd_all, i, 1, mixer=mixer, mixer_args=mixer_args,
                 g_final=final_norm if i == DEPTH - 1 else None)
    return x.reshape(BATCH, SEQ, D_MODEL)
```

```python
import functools
import math

import numpy as np
import jax
import jax.numpy as jnp
from jax import lax
from jax.experimental import pallas as pl
from jax.experimental.pallas import tpu as pltpu

F32 = jnp.float32
BF16 = jnp.bfloat16

D_MODEL = 1024
BATCH = 8
SEQ = 2048
DEPTH = 4
EPS = 1e-6
D_FF = 2816

GLA_HEADS = 4
GLA_DK_TOT = 512
GLA_DV_TOT = 1024
GLA_DK = 128
GLA_DV = 256
GLA_GATE_RANK = 16
GLA_TAU = 16.0

MLA_HEADS = 8
MLA_NOPE = 128
MLA_ROPE = 64
MLA_V = 128
MLA_Q_RANK = 768
MLA_KV_RANK = 256
MLA_QK = MLA_NOPE + MLA_ROPE
ROPE_THETA = 10000.0

LANES = 128
SUBLANES = 8
VMEM_CAPACITY_BYTES = 64 * 1024 * 1024
VMEM_LIMIT_BYTES = VMEM_CAPACITY_BYTES - 6 * 1024 * 1024

TOKEN_TILE = 512
FFN_CHUNK = 256
GLA_CHUNK = 64
GLA_LEVELS = int(math.log2(GLA_CHUNK))
GLA_SUPER = 256
ATTN_Q_TILE = 2048
ATTN_KEY_BLOCK = 512
MLA_HEAD_SLOT = 2 * LANES


def _params(n_axes):
    return pltpu.CompilerParams(
        dimension_semantics=("parallel",) * n_axes,
        vmem_limit_bytes=VMEM_LIMIT_BYTES)


def _dot(a, b):
    return jnp.dot(a, b, preferred_element_type=F32)


def _dot_nt(a, b):
    return lax.dot_general(a, b, (((1,), (1,)), ((), ())), preferred_element_type=F32)


def _dot_tn(a, b):
    return lax.dot_general(a, b, (((0,), (0,)), ((), ())), preferred_element_type=F32)


def _rms(x, g):
    return x * lax.rsqrt(jnp.mean(x * x, axis=-1, keepdims=True) + EPS) * g


def _row_spec(tile, width):
    return pl.BlockSpec((tile, width), lambda i: (i, 0))


def _const_spec(shape):
    return pl.BlockSpec(shape, lambda *_: (0,) * len(shape))


def _head_major_spec(heads, width):
    return pl.BlockSpec((heads, TOKEN_TILE, width), lambda i: (0, i, 0))


def _gla_gated(o_ref, r_ref, g_ref):
    g = g_ref[...]
    parts = []
    for hd in range(GLA_HEADS):
        r = r_ref[:, hd * GLA_DV:(hd + 1) * GLA_DV]
        parts.append((_rms(o_ref[hd], g) * (r * jax.nn.sigmoid(r))).astype(BF16))
    return jnp.concatenate(parts, axis=-1)


def _ffn_kernel(x_ref, g_ref, wgu_ref, wd_ref, *rest, mixer, final):
    o_ref = rest[-1]
    x = x_ref[...]
    if mixer == "mla":
        a_ref, wo_ref = rest[:2]
        heads = jnp.concatenate([a_ref[hd] for hd in range(MLA_HEADS)], axis=-1)
        x = x + _dot(heads, wo_ref[...])
    elif mixer == "gla":
        go_ref, gr_ref, gg_ref, wo_ref = rest[:4]
        x = x + _dot(_gla_gated(go_ref, gr_ref, gg_ref), wo_ref[...])
    h = _rms(x, g_ref[...]).astype(BF16)
    acc = None
    for c in range(D_FF // FFN_CHUNK):
        lo = c * FFN_CHUNK
        gate = _dot(h, wgu_ref[:, lo:lo + FFN_CHUNK].astype(BF16))
        up = _dot(h, wgu_ref[:, D_FF + lo:D_FF + lo + FFN_CHUNK].astype(BF16))
        act = (gate * jax.nn.sigmoid(gate) * up).astype(BF16)
        part = _dot(act, wd_ref[lo:lo + FFN_CHUNK, :].astype(BF16))
        acc = part if acc is None else acc + part
    y = x + 0.5 * acc
    if final:
        y = _rms(y, rest[-2][...])
    o_ref[...] = y


def _resident_spec(shape, index):
    return pl.BlockSpec(shape, lambda *_: index, pipeline_mode=pl.Buffered(1))


def _ffn(x, g, wgu_all, wd_all, layer, half, mixer=None, mixer_args=(), g_final=None):
    t = x.shape[0]
    final = g_final is not None
    pick = (layer, half, 0, 0)
    in_specs = [_row_spec(TOKEN_TILE, D_MODEL), _const_spec((1, D_MODEL)),
                _resident_spec((None, None, D_MODEL, 2 * D_FF), pick),
                _resident_spec((None, None, D_FF, D_MODEL), pick)]
    args = [x, g.reshape(1, D_MODEL), wgu_all, wd_all]
    if mixer == "mla":
        a, w_out = mixer_args
        in_specs += [_head_major_spec(MLA_HEADS, MLA_V), _resident_spec(w_out.shape, (0, 0))]
        args += [a, w_out]
    elif mixer == "gla":
        o, r, g_head, w_out = mixer_args
        in_specs += [_head_major_spec(GLA_HEADS, GLA_DV), _row_spec(TOKEN_TILE, GLA_DV_TOT),
                     _const_spec((1, GLA_DV)), _resident_spec(w_out.shape, (0, 0))]
        args += [o, r, g_head.reshape(1, GLA_DV), w_out]
    if final:
        in_specs.append(_const_spec((1, D_MODEL)))
        args.append(g_final.reshape(1, D_MODEL))
    return pl.pallas_call(
        functools.partial(_ffn_kernel, mixer=mixer, final=final),
        grid=(t // TOKEN_TILE,),
        in_specs=in_specs,
        out_specs=_row_spec(TOKEN_TILE, D_MODEL),
        out_shape=jax.ShapeDtypeStruct((t, D_MODEL), F32),
        compiler_params=_params(1),
        name="ffn" + ("_" + mixer if mixer else "") + ("_final" if final else ""),
    )(*args)


def _gla_in_kernel(x_ref, g_ref, w_ref, wg_ref, w2_ref, b2_ref,
                   q_ref, k_ref, v_ref, r_ref, lf_ref, lb_ref):
    h = _rms(x_ref[...], g_ref[...]).astype(BF16)
    y = _dot(h, w_ref[...])
    r_ref[...] = y[:, 2 * GLA_DK_TOT + GLA_DV_TOT:]
    low = _dot(h, wg_ref[...]).astype(BF16)
    z = _dot(low, w2_ref[...]) + b2_ref[...]
    log_a = (jnp.minimum(z, 0.0) - jnp.log1p(jnp.exp(-jnp.abs(z)))) * (math.log2(math.e) / GLA_TAU)
    for hd in range(GLA_HEADS):
        dk = slice(hd * GLA_DK, (hd + 1) * GLA_DK)
        q_ref[hd] = y[:, dk] * (GLA_DK ** -0.5)
        k_ref[hd] = y[:, GLA_DK_TOT + hd * GLA_DK:GLA_DK_TOT + (hd + 1) * GLA_DK]
        v_ref[hd] = y[:, 2 * GLA_DK_TOT + hd * GLA_DV:2 * GLA_DK_TOT + (hd + 1) * GLA_DV].astype(BF16)
        lf_ref[hd] = log_a[:, dk]
        lb_ref[hd] = log_a[:, GLA_DK_TOT + hd * GLA_DK:GLA_DK_TOT + (hd + 1) * GLA_DK]


def _gla_in(x, g, w_main, w_gate1, w_gate2, b_gate):
    t = x.shape[0]
    n_main = 2 * GLA_DK_TOT + 2 * GLA_DV_TOT
    per_head = lambda width, dtype: jax.ShapeDtypeStruct((GLA_HEADS, t, width), dtype)
    out_shape = (per_head(GLA_DK, F32), per_head(GLA_DK, F32), per_head(GLA_DV, BF16),
                 jax.ShapeDtypeStruct((t, GLA_DV_TOT), F32), per_head(GLA_DK, F32), per_head(GLA_DK, F32))
    out_spec = lambda s: (_row_spec(TOKEN_TILE, s.shape[1]) if len(s.shape) == 2 else
                          _head_major_spec(GLA_HEADS, s.shape[2]))
    return pl.pallas_call(
        _gla_in_kernel,
        grid=(t // TOKEN_TILE,),
        in_specs=[_row_spec(TOKEN_TILE, D_MODEL), _const_spec((1, D_MODEL)),
                  _const_spec((D_MODEL, n_main)), _const_spec((D_MODEL, 2 * GLA_GATE_RANK)),
                  _const_spec((2 * GLA_GATE_RANK, 2 * GLA_DK_TOT)), _const_spec((1, 2 * GLA_DK_TOT))],
        out_specs=tuple(out_spec(s) for s in out_shape),
        out_shape=out_shape,
        compiler_params=_params(1),
        name="gla_in",
    )(x, g.reshape(1, D_MODEL), w_main, w_gate1, w_gate2, b_gate)


def _gla_constants():
    ch, nl, sup = GLA_CHUNK, GLA_LEVELS, GLA_SUPER
    tri = np.tril(np.ones((ch, ch), np.float32))
    eye = np.eye(sup // ch, dtype=np.float32)
    cum_fw = np.kron(eye, tri)
    cum_bw = np.kron(eye, tri.T)
    mask = np.zeros((2 * nl + 1, ch, ch), np.float32)
    t = np.arange(ch)[:, None]
    s = np.arange(ch)[None, :]
    for lvl in range(nl):
        w = 1 << lvl
        sel = (t // (2 * w) == s // (2 * w)) & (t % (2 * w) >= w) & (s % (2 * w) < w)
        mask[lvl] = sel
        mask[nl + 1 + lvl] = sel.T
    mask[nl] = np.eye(ch, dtype=np.float32)
    rows = np.arange(ch)
    sign = np.stack([np.where(rows % (2 << lvl) >= (1 << lvl), 1.0, -1.0) for lvl in range(nl)])
    sign[0, 1::2] = 0.0
    sign = np.broadcast_to(sign[:, :, None], (nl, ch, LANES)).astype(np.float32)
    sign = np.concatenate([sign, -sign], axis=0)
    return (jnp.asarray(cum_fw, BF16), jnp.asarray(cum_bw, BF16), jnp.asarray(mask, F32),
            jnp.asarray(sign, F32))


def _level_reference(b, lvl):
    ch = GLA_CHUNK
    w = 1 << lvl
    if lvl == 0:
        return pltpu.roll(b, ch - 1, axis=0)
    if 2 * w >= 2 * SUBLANES:
        parts = [jnp.broadcast_to(b[blk + w:blk + w + 1, :], (2 * w, LANES))
                 for blk in range(0, ch, 2 * w)]
        return parts[0] if len(parts) == 1 else jnp.concatenate(parts, axis=0)
    b3 = b.reshape(ch // SUBLANES, SUBLANES, LANES)
    sub = lax.broadcasted_iota(jnp.int32, b3.shape, 1)
    ref = None
    for blk in range(SUBLANES - 2 * w, -1, -2 * w):
        row = jnp.broadcast_to(b3[:, blk + w:blk + w + 1, :], b3.shape)
        ref = row if ref is None else jnp.where(sub < blk + 2 * w, row, ref)
    return ref.reshape(ch, LANES)


def _gla_core_kernel(q_ref, k_ref, v_ref, lf_ref, lb_ref, cf_ref, cb_ref, mask_ref, sign_ref,
                     o_ref, sf_ref, sb_ref):
    ch, nl, sup = GLA_CHUNK, GLA_LEVELS, GLA_SUPER
    n_super = SEQ // sup
    sf_ref[...] = jnp.zeros_like(sf_ref)
    sb_ref[...] = jnp.zeros_like(sb_ref)
    o_ref[...] = jnp.zeros_like(o_ref)

    def cumulative(row0, la_ref, cum_ref):
        la = la_ref[pl.ds(row0, sup), :]
        la_hi = la.astype(BF16)
        la_lo = (la - la_hi.astype(F32)).astype(BF16)
        return _dot(cum_ref[...], la_hi) + _dot(cum_ref[...], la_lo)

    def intra(row0, j, cum, forward):
        rows = pl.ds(row0 + j * ch, ch)
        q = q_ref[rows, :]
        k = k_ref[rows, :]
        v = v_ref[rows, :]
        b = cum[j * ch:(j + 1) * ch]
        edge = b[ch - 1:ch, :] if forward else b[0:1, :]
        p = None
        for lvl in range(nl):
            sign = sign_ref[lvl if forward else nl + lvl]
            x = jnp.exp2((b - _level_reference(b, lvl)) * sign)
            sc = _dot_nt((q * x).astype(BF16), (k * x).astype(BF16))
            sc = sc * mask_ref[lvl if forward else nl + 1 + lvl]
            p = sc if p is None else p + sc
        if forward:
            p = p + _dot_nt(q.astype(BF16), k.astype(BF16)) * mask_ref[nl]
        qd = (q * jnp.exp2(b)).astype(BF16)
        kd = (k * jnp.exp2(edge - b)).astype(BF16)
        decay = jnp.broadcast_to(jnp.exp2(edge), (GLA_DK, GLA_DK)).T
        return rows, jnp.concatenate([qd, p.astype(BF16)], axis=1), v, decay, _dot_tn(kd, v)

    def inter(st_ref, rows, qd_p, v, decay, update):
        st = st_ref[...]
        o_ref[rows, :] += _dot(qd_p, jnp.concatenate([st.astype(BF16), v], axis=0))
        st_ref[...] = st * jnp.concatenate([decay, decay], axis=1) + update

    def body(i, carry):
        n = sup // ch
        row_f = pl.multiple_of(i * sup, sup)
        row_b = pl.multiple_of((n_super - 1 - i) * sup, sup)
        cum_f = cumulative(row_f, lf_ref, cf_ref)
        cum_b = cumulative(row_b, lb_ref, cb_ref)
        done = []
        for j in range(n):
            done.append((sf_ref, intra(row_f, j, cum_f, True)))
            done.append((sb_ref, intra(row_b, n - 1 - j, cum_b, False)))
        for st_ref, parts in done:
            inter(st_ref, *parts)
        return carry

    lax.fori_loop(0, n_super, body, 0)


def _gla_core(q, k, v, lf, lb):
    t = q.shape[1]
    cum_fw, cum_bw, mask, sign = _gla_constants()
    seq_blk = lambda width: pl.BlockSpec((None, SEQ, width), lambda b, h: (h, b, 0))
    return pl.pallas_call(
        _gla_core_kernel,
        grid=(BATCH, GLA_HEADS),
        in_specs=[seq_blk(GLA_DK), seq_blk(GLA_DK), seq_blk(GLA_DV), seq_blk(GLA_DK), seq_blk(GLA_DK),
                  _const_spec(cum_fw.shape), _const_spec(cum_bw.shape), _const_spec(mask.shape),
                  _const_spec(sign.shape)],
        out_specs=seq_blk(GLA_DV),
        out_shape=jax.ShapeDtypeStruct((GLA_HEADS, t, GLA_DV), F32),
        scratch_shapes=[pltpu.VMEM((GLA_DK, GLA_DV), F32), pltpu.VMEM((GLA_DK, GLA_DV), F32)],
        compiler_params=_params(2),
        name="gla_core",
    )(q, k, v, lf, lb, cum_fw, cum_bw, mask, sign)


def _rope_table_kernel(pos_ref, freq_ref, cos_ref, sin_ref):
    ang = pos_ref[...].astype(F32) * freq_ref[...]
    cos_ref[...] = jnp.cos(ang)
    sin_ref[...] = jnp.sin(ang)


def _rope_tables(positions):
    half = MLA_ROPE // 2
    per_row = LANES // half
    t = positions.size
    inv_freq = 1.0 / (ROPE_THETA ** (jnp.arange(0, MLA_ROPE, 2, dtype=F32) / MLA_ROPE))
    pos = jnp.repeat(positions.reshape(t // per_row, per_row), half, axis=1)
    rows = t // per_row
    tile = 512
    cos, sin = pl.pallas_call(
        _rope_table_kernel,
        grid=(rows // tile,),
        in_specs=[_row_spec(tile, LANES), _const_spec((1, LANES))],
        out_specs=(_row_spec(tile, LANES), _row_spec(tile, LANES)),
        out_shape=(jax.ShapeDtypeStruct((rows, LANES), F32),) * 2,
        compiler_params=_params(1),
        name="rope_table",
    )(pos, jnp.tile(inv_freq, per_row).reshape(1, LANES))
    widen = lambda a: jnp.tile(a.reshape(t, half), (1, per_row))
    return widen(cos), widen(sin)


def _mla_in_kernel(x_ref, g_ref, wcq_ref, wckv_ref, wkpe_ref, gq_ref, gkv_ref,
                   wqn_ref, wqp_ref, wqpr_ref, wkn_ref, wv_ref, cos_ref, sin_ref,
                   q_out, k_out, vt_out):
    h = _rms(x_ref[...], g_ref[...]).astype(BF16)
    cos = cos_ref[...]
    sin = sin_ref[...]
    kpe2 = _dot(h, wkpe_ref[...])
    kpe = (kpe2[:, :LANES] * cos + kpe2[:, LANES:] * sin).astype(BF16)
    hq = _rms(_dot(h, wcq_ref[...]), gq_ref[...]).astype(BF16)
    hkv = _rms(_dot(h, wckv_ref[...]), gkv_ref[...]).astype(BF16)
    scale = MLA_QK ** -0.5 * math.log2(math.e)
    qn = _dot(hq, wqn_ref[...])
    pairs = MLA_HEADS // 2
    cos4 = jnp.concatenate([cos] * pairs, axis=-1)
    sin4 = jnp.concatenate([sin] * pairs, axis=-1)
    qp = (_dot(hq, wqp_ref[...]) * cos4 + _dot(hq, wqpr_ref[...]) * sin4) * scale
    kn = _dot(hkv, wkn_ref[...])
    vt_out[...] = _dot(hkv, wv_ref[...]).T.astype(BF16)
    low_half = lax.broadcasted_iota(jnp.int32, (qp.shape[0], LANES), 1) < MLA_ROPE
    for hd in range(MLA_HEADS):
        cols = slice(hd * LANES, (hd + 1) * LANES)
        pair = qp[:, (hd // 2) * LANES:(hd // 2 + 1) * LANES]
        if hd % 2:
            pair = pltpu.roll(pair, MLA_ROPE, axis=1)
        q_out[hd, :, :LANES] = (qn[:, cols] * scale).astype(BF16)
        q_out[hd, :, LANES:] = jnp.where(low_half, pair, 0.0).astype(BF16)
        k_out[hd, :, :LANES] = kn[:, cols].astype(BF16)
        k_out[hd, :, LANES:] = kpe


def _mla_in(x, g, w, cos, sin):
    t = x.shape[0]
    weights = [w["wcq"], w["wckv"], w["wkpe"], w["gq"], w["gkv"],
               w["wqn"], w["wqp"], w["wqpr"], w["wkn"], w["wv"]]
    slots = jax.ShapeDtypeStruct((MLA_HEADS, t, MLA_HEAD_SLOT), BF16)
    out_shape = (slots, slots, jax.ShapeDtypeStruct((MLA_HEADS * MLA_V, t), BF16))
    return pl.pallas_call(
        _mla_in_kernel,
        grid=(t // TOKEN_TILE,),
        in_specs=[_row_spec(TOKEN_TILE, D_MODEL), _const_spec((1, D_MODEL))]
                 + [_const_spec(a.shape) for a in weights]
                 + [_row_spec(TOKEN_TILE, LANES), _row_spec(TOKEN_TILE, LANES)],
        out_specs=(_head_major_spec(MLA_HEADS, MLA_HEAD_SLOT), _head_major_spec(MLA_HEADS, MLA_HEAD_SLOT),
                   pl.BlockSpec((MLA_HEADS * MLA_V, TOKEN_TILE), lambda i: (0, i))),
        out_shape=out_shape,
        compiler_params=_params(1),
        name="mla_in",
    )(x, g.reshape(1, D_MODEL), *weights, cos, sin)


def _mla_attn_kernel(q_ref, k_ref, vt_ref, o_ref, s_ref):
    n_blocks = SEQ // ATTN_KEY_BLOCK

    def scores(j):
        keys = slice(j * ATTN_KEY_BLOCK, (j + 1) * ATTN_KEY_BLOCK)
        s_ref[j % 2] = _dot_nt(k_ref[keys, :], q_ref[...])

    scores(0)
    ones = jnp.ones((2 * SUBLANES, ATTN_KEY_BLOCK), BF16)
    m = acc = None
    for j in range(n_blocks):
        keys = slice(j * ATTN_KEY_BLOCK, (j + 1) * ATTN_KEY_BLOCK)
        if j + 1 < n_blocks:
            scores(j + 1)
        s = s_ref[j % 2]
        m_blk = jnp.max(s, axis=0, keepdims=True)
        vt_ones = jnp.concatenate([vt_ref[:, keys], ones], axis=0)
        if j == 0:
            m = m_blk
            acc = _dot(vt_ones, jnp.exp2(s - m).astype(BF16))
        else:
            m_new = jnp.maximum(m, m_blk)
            acc = jnp.exp2(m - m_new) * acc + _dot(vt_ones, jnp.exp2(s - m_new).astype(BF16))
            m = m_new
    o_ref[...] = (acc[:MLA_V] / acc[MLA_V:MLA_V + 1]).T.astype(o_ref.dtype)


def _mla_attn(q, k, vt):
    t = q.shape[1]
    n_q = SEQ // ATTN_Q_TILE
    return pl.pallas_call(
        _mla_attn_kernel,
        grid=(BATCH, MLA_HEADS, n_q),
        in_specs=[pl.BlockSpec((None, ATTN_Q_TILE, MLA_HEAD_SLOT), lambda b, h, i: (h, b * n_q + i, 0)),
                  pl.BlockSpec((None, SEQ, MLA_HEAD_SLOT), lambda b, h, i: (h, b, 0)),
                  pl.BlockSpec((MLA_V, SEQ), lambda b, h, i: (h, b))],
        out_specs=pl.BlockSpec((None, ATTN_Q_TILE, MLA_V), lambda b, h, i: (h, b * n_q + i, 0)),
        out_shape=jax.ShapeDtypeStruct((MLA_HEADS, t, MLA_V), BF16),
        scratch_shapes=[pltpu.VMEM((2, ATTN_KEY_BLOCK, ATTN_Q_TILE), F32)],
        compiler_params=_params(3),
        name="mla_attn",
    )(q, k, vt)


def _rotate_half_columns(w):
    shape = w.shape
    w = w.reshape(shape[0], -1, 2, MLA_ROPE // 2)
    return jnp.stack([-w[:, :, 1], w[:, :, 0]], axis=2).reshape(shape)


def _pad_rope_columns(w):
    k = w.shape[0]
    w = w.reshape(k, -1, MLA_ROPE)
    return jnp.pad(w, ((0, 0), (0, 0), (0, LANES - MLA_ROPE))).reshape(k, -1)


def _mla_weights(w_in, q_norm, kv_norm, w_uq, w_ukv):
    w_kpe = w_in[:, MLA_Q_RANK + MLA_KV_RANK:]
    w_uq = w_uq.reshape(MLA_Q_RANK, MLA_HEADS, MLA_QK)
    w_uq_rope = w_uq[:, :, MLA_NOPE:].reshape(MLA_Q_RANK, MLA_HEADS * MLA_ROPE)
    w_ukv = w_ukv.reshape(MLA_KV_RANK, MLA_HEADS, MLA_NOPE + MLA_V)
    return {
        "wcq": w_in[:, :MLA_Q_RANK].astype(BF16),
        "wckv": w_in[:, MLA_Q_RANK:MLA_Q_RANK + MLA_KV_RANK].astype(BF16),
        "wkpe": jnp.concatenate([_pad_rope_columns(w_kpe),
                                 _pad_rope_columns(_rotate_half_columns(w_kpe))], axis=1).astype(BF16),
        "gq": q_norm.reshape(1, MLA_Q_RANK),
        "gkv": kv_norm.reshape(1, MLA_KV_RANK),
        "wqn": w_uq[:, :, :MLA_NOPE].reshape(MLA_Q_RANK, MLA_HEADS * MLA_NOPE).astype(BF16),
        "wqp": w_uq_rope.astype(BF16),
        "wqpr": _rotate_half_columns(w_uq_rope).astype(BF16),
        "wkn": w_ukv[:, :, :MLA_NOPE].reshape(MLA_KV_RANK, MLA_HEADS * MLA_NOPE).astype(BF16),
        "wv": w_ukv[:, :, MLA_NOPE:].reshape(MLA_KV_RANK, MLA_HEADS * MLA_V).astype(BF16),
    }


def kernel(x, positions, ffn_norm, ffn_w_gu, ffn_w_down, mix_norm, gla_w_in, gla_w_gate2, gla_b_gate,
           gla_head_norm, gla_w_out, mla_w_in, mla_q_norm, mla_kv_norm, mla_w_uq, mla_w_ukv, mla_w_out,
           final_norm):
    assert x.shape == (BATCH, SEQ, D_MODEL)
    t = BATCH * SEQ
    x = x.reshape(t, D_MODEL)
    cos, sin = _rope_tables(positions)
    n_main = 2 * GLA_DK_TOT + 2 * GLA_DV_TOT
    zeros = jnp.zeros((GLA_GATE_RANK, GLA_DK_TOT), F32)
    wgu_all = ffn_w_gu
    wd_all = ffn_w_down
    for i in range(DEPTH):
        x = _ffn(x, ffn_norm[i, 0], wgu_all, wd_all, i, 0)
        j = i // 2
        if i % 2 == 0:
            w_gate2 = jnp.block([[gla_w_gate2[j, 0], zeros], [zeros, gla_w_gate2[j, 1]]]).astype(BF16)
            q, k, v, r, lf, lb = _gla_in(
                x, mix_norm[i], gla_w_in[j, :, :n_main].astype(BF16), gla_w_in[j, :, n_main:].astype(BF16),
                w_gate2, gla_b_gate[j].reshape(1, 2 * GLA_DK_TOT))
            o = _gla_core(q, k, v, lf, lb)
            mixer, mixer_args = "gla", (o, r, gla_head_norm[j], gla_w_out[j].astype(BF16))
        else:
            w = _mla_weights(mla_w_in[j], mla_q_norm[j], mla_kv_norm[j], mla_w_uq[j], mla_w_ukv[j])
            qf, kf, vt = _mla_in(x, mix_norm[i], w, cos, sin)
            mixer, mixer_args = "mla", (_mla_attn(qf, kf, vt), mla_w_out[j].astype(BF16))
        x = _ffn(x, ffn_norm[i, 1], wgu_all, wd_all, i, 1, mixer=mixer, mixer_args=mixer_args,
                 g_final=final_norm if i == DEPTH - 1 else None)
    return x.reshape(BATCH, SEQ, D_MODEL)
```

```python
import functools
import math

import numpy as np
import jax
import jax.numpy as jnp
from jax import lax
from jax.experimental import pallas as pl
from jax.experimental.pallas import tpu as pltpu

F32 = jnp.float32
BF16 = jnp.bfloat16

D_MODEL = 1024
BATCH = 8
SEQ = 2048
DEPTH = 4
EPS = 1e-6
D_FF = 2816

GLA_HEADS = 4
GLA_DK_TOT = 512
GLA_DV_TOT = 1024
GLA_DK = 128
GLA_DV = 256
GLA_GATE_RANK = 16
GLA_TAU = 16.0

MLA_HEADS = 8
MLA_NOPE = 128
MLA_ROPE = 64
MLA_V = 128
MLA_Q_RANK = 768
MLA_KV_RANK = 256
MLA_QK = MLA_NOPE + MLA_ROPE
ROPE_THETA = 10000.0

LANES = 128
SUBLANES = 8
VMEM_CAPACITY_BYTES = 64 * 1024 * 1024
VMEM_LIMIT_BYTES = VMEM_CAPACITY_BYTES - 6 * 1024 * 1024

TOKEN_TILE = 512
FFN_CHUNK = 256
GLA_CHUNK = 64
GLA_LEVELS = int(math.log2(GLA_CHUNK))
GLA_CUM_ROWS = 256
GLA_SUPER = 512
ATTN_Q_TILE = 2048
ATTN_KEY_BLOCK = 512
MLA_HEAD_SLOT = 2 * LANES


def _params(n_axes):
    return pltpu.CompilerParams(
        dimension_semantics=("parallel",) * n_axes,
        vmem_limit_bytes=VMEM_LIMIT_BYTES)


def _dot(a, b):
    return jnp.dot(a, b, preferred_element_type=F32)


def _dot_nt(a, b):
    return lax.dot_general(a, b, (((1,), (1,)), ((), ())), preferred_element_type=F32)


def _dot_tn(a, b):
    return lax.dot_general(a, b, (((0,), (0,)), ((), ())), preferred_element_type=F32)


def _rms(x, g):
    return x * lax.rsqrt(jnp.mean(x * x, axis=-1, keepdims=True) + EPS) * g


def _row_spec(tile, width):
    return pl.BlockSpec((tile, width), lambda i: (i, 0))


def _const_spec(shape):
    return pl.BlockSpec(shape, lambda *_: (0,) * len(shape))


def _head_major_spec(heads, width):
    return pl.BlockSpec((heads, TOKEN_TILE, width), lambda i: (0, i, 0))


def _gla_gated(o_ref, r_ref, g_ref):
    g = g_ref[...]
    parts = []
    for hd in range(GLA_HEADS):
        r = r_ref[:, hd * GLA_DV:(hd + 1) * GLA_DV]
        parts.append((_rms(o_ref[hd], g) * (r * jax.nn.sigmoid(r))).astype(BF16))
    return jnp.concatenate(parts, axis=-1)


def _ffn_kernel(x_ref, g_ref, wgu_ref, wd_ref, *rest, mixer, final):
    o_ref = rest[-1]
    x = x_ref[...]
    if mixer == "mla":
        a_ref, wo_ref = rest[:2]
        heads = jnp.concatenate([a_ref[hd] for hd in range(MLA_HEADS)], axis=-1)
        x = x + _dot(heads, wo_ref[...])
    elif mixer == "gla":
        go_ref, gr_ref, gg_ref, wo_ref = rest[:4]
        x = x + _dot(_gla_gated(go_ref, gr_ref, gg_ref), wo_ref[...])
    h = _rms(x, g_ref[...]).astype(BF16)
    acc = None
    for c in range(D_FF // FFN_CHUNK):
        lo = c * FFN_CHUNK
        gate = _dot(h, wgu_ref[:, lo:lo + FFN_CHUNK].astype(BF16))
        up = _dot(h, wgu_ref[:, D_FF + lo:D_FF + lo + FFN_CHUNK].astype(BF16))
        act = (gate * jax.nn.sigmoid(gate) * up).astype(BF16)
        part = _dot(act, wd_ref[lo:lo + FFN_CHUNK, :].astype(BF16))
        acc = part if acc is None else acc + part
    y = x + 0.5 * acc
    if final:
        y = _rms(y, rest[-2][...])
    o_ref[...] = y


def _resident_spec(shape, index):
    return pl.BlockSpec(shape, lambda *_: index, pipeline_mode=pl.Buffered(1))


def _ffn(x, g, wgu_all, wd_all, layer, half, mixer=None, mixer_args=(), g_final=None):
    t = x.shape[0]
    final = g_final is not None
    pick = (layer, half, 0, 0)
    in_specs = [_row_spec(TOKEN_TILE, D_MODEL), _const_spec((1, D_MODEL)),
                _resident_spec((None, None, D_MODEL, 2 * D_FF), pick),
                _resident_spec((None, None, D_FF, D_MODEL), pick)]
    args = [x, g.reshape(1, D_MODEL), wgu_all, wd_all]
    if mixer == "mla":
        a, w_out = mixer_args
        in_specs += [_head_major_spec(MLA_HEADS, MLA_V), _resident_spec(w_out.shape, (0, 0))]
        args += [a, w_out]
    elif mixer == "gla":
        o, r, g_head, w_out = mixer_args
        in_specs += [_head_major_spec(GLA_HEADS, GLA_DV), _row_spec(TOKEN_TILE, GLA_DV_TOT),
                     _const_spec((1, GLA_DV)), _resident_spec(w_out.shape, (0, 0))]
        args += [o, r, g_head.reshape(1, GLA_DV), w_out]
    if final:
        in_specs.append(_const_spec((1, D_MODEL)))
        args.append(g_final.reshape(1, D_MODEL))
    return pl.pallas_call(
        functools.partial(_ffn_kernel, mixer=mixer, final=final),
        grid=(t // TOKEN_TILE,),
        in_specs=in_specs,
        out_specs=_row_spec(TOKEN_TILE, D_MODEL),
        out_shape=jax.ShapeDtypeStruct((t, D_MODEL), F32),
        compiler_params=_params(1),
        name="ffn" + ("_" + mixer if mixer else "") + ("_final" if final else ""),
    )(*args)


def _gla_in_kernel(x_ref, g_ref, w_ref, wg_ref, w2_ref, b2_ref,
                   q_ref, k_ref, v_ref, r_ref, lf_ref, lb_ref):
    h = _rms(x_ref[...], g_ref[...]).astype(BF16)
    low = _dot(h, wg_ref[...]).astype(BF16)
    z = _dot(low, w2_ref[...]) + b2_ref[...]
    log_a = (jnp.minimum(z, 0.0) - jnp.log(1.0 + jnp.exp(-jnp.abs(z)))) * (math.log2(math.e) / GLA_TAU)
    for hd in range(GLA_HEADS):
        lf_ref[hd] = log_a[:, hd * GLA_DK:(hd + 1) * GLA_DK]
        lb_ref[hd] = log_a[:, GLA_DK_TOT + hd * GLA_DK:GLA_DK_TOT + (hd + 1) * GLA_DK]
    qk = _dot(h, w_ref[:, :2 * GLA_DK_TOT])
    for hd in range(GLA_HEADS):
        q_ref[hd] = qk[:, hd * GLA_DK:(hd + 1) * GLA_DK] * (GLA_DK ** -0.5)
        k_ref[hd] = qk[:, GLA_DK_TOT + hd * GLA_DK:GLA_DK_TOT + (hd + 1) * GLA_DK]
    v = _dot(h, w_ref[:, 2 * GLA_DK_TOT:2 * GLA_DK_TOT + GLA_DV_TOT])
    for hd in range(GLA_HEADS):
        v_ref[hd] = v[:, hd * GLA_DV:(hd + 1) * GLA_DV].astype(BF16)
    r_ref[...] = _dot(h, w_ref[:, 2 * GLA_DK_TOT + GLA_DV_TOT:])


def _gla_in(x, g, w_main, w_gate1, w_gate2, b_gate):
    t = x.shape[0]
    n_main = 2 * GLA_DK_TOT + 2 * GLA_DV_TOT
    per_head = lambda width, dtype: jax.ShapeDtypeStruct((GLA_HEADS, t, width), dtype)
    out_shape = (per_head(GLA_DK, F32), per_head(GLA_DK, F32), per_head(GLA_DV, BF16),
                 jax.ShapeDtypeStruct((t, GLA_DV_TOT), F32), per_head(GLA_DK, F32), per_head(GLA_DK, F32))
    out_spec = lambda s: (_row_spec(TOKEN_TILE, s.shape[1]) if len(s.shape) == 2 else
                          _head_major_spec(GLA_HEADS, s.shape[2]))
    return pl.pallas_call(
        _gla_in_kernel,
        grid=(t // TOKEN_TILE,),
        in_specs=[_row_spec(TOKEN_TILE, D_MODEL), _const_spec((1, D_MODEL)),
                  _const_spec((D_MODEL, n_main)), _const_spec((D_MODEL, 2 * GLA_GATE_RANK)),
                  _const_spec((2 * GLA_GATE_RANK, 2 * GLA_DK_TOT)), _const_spec((1, 2 * GLA_DK_TOT))],
        out_specs=tuple(out_spec(s) for s in out_shape),
        out_shape=out_shape,
        compiler_params=_params(1),
        name="gla_in",
    )(x, g.reshape(1, D_MODEL), w_main, w_gate1, w_gate2, b_gate)


def _gla_constants():
    ch, nl = GLA_CHUNK, GLA_LEVELS
    tri = np.tril(np.ones((ch, ch), np.float32))
    eye = np.eye(GLA_CUM_ROWS // ch, dtype=np.float32)
    cum_fw = np.kron(eye, tri)
    cum_bw = np.kron(eye, tri.T)
    mask = np.zeros((2 * nl + 1, ch, ch), np.float32)
    t = np.arange(ch)[:, None]
    s = np.arange(ch)[None, :]
    for lvl in range(nl):
        w = 1 << lvl
        sel = (t // (2 * w) == s // (2 * w)) & (t % (2 * w) >= w) & (s % (2 * w) < w)
        mask[lvl] = sel
        mask[nl + 1 + lvl] = sel.T
    mask[nl] = np.eye(ch, dtype=np.float32)
    rows = np.arange(ch)
    sign = np.stack([np.where(rows % (2 << lvl) >= (1 << lvl), 1.0, -1.0) for lvl in range(nl)])
    sign[0, 1::2] = 0.0
    sign = np.broadcast_to(sign[:, :, None], (nl, ch, LANES)).astype(np.float32)
    sign = np.concatenate([sign, -sign], axis=0)
    return (jnp.asarray(cum_fw, BF16), jnp.asarray(cum_bw, BF16), jnp.asarray(mask, F32),
            jnp.asarray(sign, F32))


def _level_reference(b, lvl):
    ch = GLA_CHUNK
    w = 1 << lvl
    if lvl == 0:
        return pltpu.roll(b, ch - 1, axis=0)
    if 2 * w >= 2 * SUBLANES:
        parts = [jnp.broadcast_to(b[blk + w:blk + w + 1, :], (2 * w, LANES))
                 for blk in range(0, ch, 2 * w)]
        return parts[0] if len(parts) == 1 else jnp.concatenate(parts, axis=0)
    b3 = b.reshape(ch // SUBLANES, SUBLANES, LANES)
    sub = lax.broadcasted_iota(jnp.int32, b3.shape, 1)
    ref = None
    for blk in range(SUBLANES - 2 * w, -1, -2 * w):
        row = jnp.broadcast_to(b3[:, blk + w:blk + w + 1, :], b3.shape)
        ref = row if ref is None else jnp.where(sub < blk + 2 * w, row, ref)
    return ref.reshape(ch, LANES)


def _gla_core_kernel(q_ref, k_ref, v_ref, lf_ref, lb_ref, cf_ref, cb_ref, mask_ref, sign_ref,
                     o_ref, sf_ref, sb_ref):
    ch, nl, sup = GLA_CHUNK, GLA_LEVELS, GLA_SUPER
    n_super = SEQ // sup
    sf_ref[...] = jnp.zeros_like(sf_ref)
    sb_ref[...] = jnp.zeros_like(sb_ref)
    o_ref[...] = jnp.zeros_like(o_ref)

    def cumulative(row0, la_ref, cum_ref):
        pieces = []
        for r in range(0, sup, GLA_CUM_ROWS):
            la = la_ref[pl.ds(row0 + r, GLA_CUM_ROWS), :]
            la_hi = la.astype(BF16)
            la_lo = (la - la_hi.astype(F32)).astype(BF16)
            pieces.append(_dot(cum_ref[...], la_hi) + _dot(cum_ref[...], la_lo))
        return jnp.concatenate(pieces, axis=0)

    def intra(row0, j, cum, forward):
        rows = pl.ds(row0 + j * ch, ch)
        q = q_ref[rows, :]
        k = k_ref[rows, :]
        v = v_ref[rows, :]
        b = cum[j * ch:(j + 1) * ch]
        edge = b[ch - 1:ch, :] if forward else b[0:1, :]
        p = None
        for lvl in range(nl):
            sign = sign_ref[lvl if forward else nl + lvl]
            x = jnp.exp2((b - _level_reference(b, lvl)) * sign)
            sc = _dot_nt((q * x).astype(BF16), (k * x).astype(BF16))
            sc = sc * mask_ref[lvl if forward else nl + 1 + lvl]
            p = sc if p is None else p + sc
        if forward:
            p = p + _dot_nt(q.astype(BF16), k.astype(BF16)) * mask_ref[nl]
        qd = (q * jnp.exp2(b)).astype(BF16)
        kd = (k * jnp.exp2(edge - b)).astype(BF16)
        decay = jnp.broadcast_to(jnp.exp2(edge), (GLA_DK, GLA_DK)).T
        return rows, jnp.concatenate([qd, p.astype(BF16)], axis=1), v, decay, _dot_tn(kd, v)

    def inter(st_ref, rows, qd_p, v, decay, update):
        st = st_ref[...]
        o_ref[rows, :] += _dot(qd_p, jnp.concatenate([st.astype(BF16), v], axis=0))
        st_ref[...] = st * jnp.concatenate([decay, decay], axis=1) + update

    def body(i, carry):
        n = sup // ch
        row_f = pl.multiple_of(i * sup, sup)
        row_b = pl.multiple_of((n_super - 1 - i) * sup, sup)
        cum_f = cumulative(row_f, lf_ref, cf_ref)
        cum_b = cumulative(row_b, lb_ref, cb_ref)
        done = []
        for j in range(n):
            done.append((sf_ref, intra(row_f, j, cum_f, True)))
            done.append((sb_ref, intra(row_b, n - 1 - j, cum_b, False)))
        for st_ref, parts in done:
            inter(st_ref, *parts)
        return carry

    lax.fori_loop(0, n_super, body, 0)


def _gla_core(q, k, v, lf, lb):
    t = q.shape[1]
    cum_fw, cum_bw, mask, sign = _gla_constants()
    seq_blk = lambda width: pl.BlockSpec((None, SEQ, width), lambda b, h: (h, b, 0))
    return pl.pallas_call(
        _gla_core_kernel,
        grid=(BATCH, GLA_HEADS),
        in_specs=[seq_blk(GLA_DK), seq_blk(GLA_DK), seq_blk(GLA_DV), seq_blk(GLA_DK), seq_blk(GLA_DK),
                  _const_spec(cum_fw.shape), _const_spec(cum_bw.shape), _const_spec(mask.shape),
                  _const_spec(sign.shape)],
        out_specs=seq_blk(GLA_DV),
        out_shape=jax.ShapeDtypeStruct((GLA_HEADS, t, GLA_DV), F32),
        scratch_shapes=[pltpu.VMEM((GLA_DK, GLA_DV), F32), pltpu.VMEM((GLA_DK, GLA_DV), F32)],
        compiler_params=_params(2),
        name="gla_core",
    )(q, k, v, lf, lb, cum_fw, cum_bw, mask, sign)


def _rope_table_kernel(pos_ref, freq_ref, cos_ref, sin_ref):
    ang = pos_ref[...].astype(F32) * freq_ref[...]
    cos_ref[...] = jnp.cos(ang)
    sin_ref[...] = jnp.sin(ang)


def _rope_tables(positions):
    half = MLA_ROPE // 2
    per_row = LANES // half
    t = positions.size
    inv_freq = 1.0 / (ROPE_THETA ** (jnp.arange(0, MLA_ROPE, 2, dtype=F32) / MLA_ROPE))
    pos = jnp.repeat(positions.reshape(t // per_row, per_row), half, axis=1)
    rows = t // per_row
    tile = 512
    cos, sin = pl.pallas_call(
        _rope_table_kernel,
        grid=(rows // tile,),
        in_specs=[_row_spec(tile, LANES), _const_spec((1, LANES))],
        out_specs=(_row_spec(tile, LANES), _row_spec(tile, LANES)),
        out_shape=(jax.ShapeDtypeStruct((rows, LANES), F32),) * 2,
        compiler_params=_params(1),
        name="rope_table",
    )(pos, jnp.tile(inv_freq, per_row).reshape(1, LANES))
    widen = lambda a: jnp.tile(a.reshape(t, half), (1, per_row))
    return widen(cos), widen(sin)


def _mla_in_kernel(x_ref, g_ref, wcq_ref, wckv_ref, wkpe_ref, gq_ref, gkv_ref,
                   wqn_ref, wqp_ref, wqpr_ref, wkn_ref, wv_ref, cos_ref, sin_ref,
                   q_out, k_out, vt_out):
    h = _rms(x_ref[...], g_ref[...]).astype(BF16)
    cos = cos_ref[...]
    sin = sin_ref[...]
    kpe2 = _dot(h, wkpe_ref[...])
    kpe = (kpe2[:, :LANES] * cos + kpe2[:, LANES:] * sin).astype(BF16)
    hq = _rms(_dot(h, wcq_ref[...]), gq_ref[...]).astype(BF16)
    hkv = _rms(_dot(h, wckv_ref[...]), gkv_ref[...]).astype(BF16)
    scale = MLA_QK ** -0.5 * math.log2(math.e)
    qn = _dot(hq, wqn_ref[...])
    pairs = MLA_HEADS // 2
    cos4 = jnp.concatenate([cos] * pairs, axis=-1)
    sin4 = jnp.concatenate([sin] * pairs, axis=-1)
    qp = (_dot(hq, wqp_ref[...]) * cos4 + _dot(hq, wqpr_ref[...]) * sin4) * scale
    kn = _dot(hkv, wkn_ref[...])
    vt_out[...] = _dot(hkv, wv_ref[...]).T.astype(BF16)
    low_half = lax.broadcasted_iota(jnp.int32, (qp.shape[0], LANES), 1) < MLA_ROPE
    for hd in range(MLA_HEADS):
        cols = slice(hd * LANES, (hd + 1) * LANES)
        pair = qp[:, (hd // 2) * LANES:(hd // 2 + 1) * LANES]
        if hd % 2:
            pair = pltpu.roll(pair, MLA_ROPE, axis=1)
        q_out[hd, :, :LANES] = (qn[:, cols] * scale).astype(BF16)
        q_out[hd, :, LANES:] = jnp.where(low_half, pair, 0.0).astype(BF16)
        k_out[hd, :, :LANES] = kn[:, cols].astype(BF16)
        k_out[hd, :, LANES:] = kpe


def _mla_in(x, g, w, cos, sin):
    t = x.shape[0]
    weights = [w["wcq"], w["wckv"], w["wkpe"], w["gq"], w["gkv"],
               w["wqn"], w["wqp"], w["wqpr"], w["wkn"], w["wv"]]
    slots = jax.ShapeDtypeStruct((MLA_HEADS, t, MLA_HEAD_SLOT), BF16)
    out_shape = (slots, slots, jax.ShapeDtypeStruct((MLA_HEADS * MLA_V, t), BF16))
    return pl.pallas_call(
        _mla_in_kernel,
        grid=(t // TOKEN_TILE,),
        in_specs=[_row_spec(TOKEN_TILE, D_MODEL), _const_spec((1, D_MODEL))]
                 + [_const_spec(a.shape) for a in weights]
                 + [_row_spec(TOKEN_TILE, LANES), _row_spec(TOKEN_TILE, LANES)],
        out_specs=(_head_major_spec(MLA_HEADS, MLA_HEAD_SLOT), _head_major_spec(MLA_HEADS, MLA_HEAD_SLOT),
                   pl.BlockSpec((MLA_HEADS * MLA_V, TOKEN_TILE), lambda i: (0, i))),
        out_shape=out_shape,
        compiler_params=_params(1),
        name="mla_in",
    )(x, g.reshape(1, D_MODEL), *weights, cos, sin)


def _mla_attn_kernel(q_ref, k_ref, vt_ref, o_ref, s_ref):
    n_blocks = SEQ // ATTN_KEY_BLOCK

    def scores(j):
        keys = slice(j * ATTN_KEY_BLOCK, (j + 1) * ATTN_KEY_BLOCK)
        s_ref[j % 2] = _dot_nt(k_ref[keys, :], q_ref[...])

    scores(0)
    ones = jnp.ones((2 * SUBLANES, ATTN_KEY_BLOCK), BF16)
    m = acc = None
    for j in range(n_blocks):
        keys = slice(j * ATTN_KEY_BLOCK, (j + 1) * ATTN_KEY_BLOCK)
        if j + 1 < n_blocks:
            scores(j + 1)
        s = s_ref[j % 2]
        m_blk = jnp.max(s, axis=0, keepdims=True)
        vt_ones = jnp.concatenate([vt_ref[:, keys], ones], axis=0)
        if j == 0:
            m = m_blk
            acc = _dot(vt_ones, jnp.exp2(s - m).astype(BF16))
        else:
            m_new = jnp.maximum(m, m_blk)
            acc = jnp.exp2(m - m_new) * acc + _dot(vt_ones, jnp.exp2(s - m_new).astype(BF16))
            m = m_new
    o_ref[...] = (acc[:MLA_V] / acc[MLA_V:MLA_V + 1]).T.astype(o_ref.dtype)


def _mla_attn(q, k, vt):
    t = q.shape[1]
    n_q = SEQ // ATTN_Q_TILE
    return pl.pallas_call(
        _mla_attn_kernel,
        grid=(BATCH, MLA_HEADS, n_q),
        in_specs=[pl.BlockSpec((None, ATTN_Q_TILE, MLA_HEAD_SLOT), lambda b, h, i: (h, b * n_q + i, 0)),
                  pl.BlockSpec((None, SEQ, MLA_HEAD_SLOT), lambda b, h, i: (h, b, 0)),
                  pl.BlockSpec((MLA_V, SEQ), lambda b, h, i: (h, b))],
        out_specs=pl.BlockSpec((None, ATTN_Q_TILE, MLA_V), lambda b, h, i: (h, b * n_q + i, 0)),
        out_shape=jax.ShapeDtypeStruct((MLA_HEADS, t, MLA_V), BF16),
        scratch_shapes=[pltpu.VMEM((2, ATTN_KEY_BLOCK, ATTN_Q_TILE), F32)],
        compiler_params=_params(3),
        name="mla_attn",
    )(q, k, vt)


def _rotate_half_columns(w):
    shape = w.shape
    w = w.reshape(shape[0], -1, 2, MLA_ROPE // 2)
    return jnp.stack([-w[:, :, 1], w[:, :, 0]], axis=2).reshape(shape)


def _pad_rope_columns(w):
    k = w.shape[0]
    w = w.reshape(k, -1, MLA_ROPE)
    return jnp.pad(w, ((0, 0), (0, 0), (0, LANES - MLA_ROPE))).reshape(k, -1)


def _mla_weights(w_in, q_norm, kv_norm, w_uq, w_ukv):
    w_kpe = w_in[:, MLA_Q_RANK + MLA_KV_RANK:]
    w_uq = w_uq.reshape(MLA_Q_RANK, MLA_HEADS, MLA_QK)
    w_uq_rope = w_uq[:, :, MLA_NOPE:].reshape(MLA_Q_RANK, MLA_HEADS * MLA_ROPE)
    w_ukv = w_ukv.reshape(MLA_KV_RANK, MLA_HEADS, MLA_NOPE + MLA_V)
    return {
        "wcq": w_in[:, :MLA_Q_RANK].astype(BF16),
        "wckv": w_in[:, MLA_Q_RANK:MLA_Q_RANK + MLA_KV_RANK].astype(BF16),
        "wkpe": jnp.concatenate([_pad_rope_columns(w_kpe),
                                 _pad_rope_columns(_rotate_half_columns(w_kpe))], axis=1).astype(BF16),
        "gq": q_norm.reshape(1, MLA_Q_RANK),
        "gkv": kv_norm.reshape(1, MLA_KV_RANK),
        "wqn": w_uq[:, :, :MLA_NOPE].reshape(MLA_Q_RANK, MLA_HEADS * MLA_NOPE).astype(BF16),
        "wqp": w_uq_rope.astype(BF16),
        "wqpr": _rotate_half_columns(w_uq_rope).astype(BF16),
        "wkn": w_ukv[:, :, :MLA_NOPE].reshape(MLA_KV_RANK, MLA_HEADS * MLA_NOPE).astype(BF16),
        "wv": w_ukv[:, :, MLA_NOPE:].reshape(MLA_KV_RANK, MLA_HEADS * MLA_V).astype(BF16),
    }


def kernel(x, positions, ffn_norm, ffn_w_gu, ffn_w_down, mix_norm, gla_w_in, gla_w_gate2, gla_b_gate,
           gla_head_norm, gla_w_out, mla_w_in, mla_q_norm, mla_kv_norm, mla_w_uq, mla_w_ukv, mla_w_out,
           final_norm):
    assert x.shape == (BATCH, SEQ, D_MODEL)
    t = BATCH * SEQ
    x = x.reshape(t, D_MODEL)
    cos, sin = _rope_tables(positions)
    n_main = 2 * GLA_DK_TOT + 2 * GLA_DV_TOT
    zeros = jnp.zeros((GLA_GATE_RANK, GLA_DK_TOT), F32)
    wgu_all = ffn_w_gu
    wd_all = ffn_w_down
    for i in range(DEPTH):
        x = _ffn(x, ffn_norm[i, 0], wgu_all, wd_all, i, 0)
        j = i // 2
        if i % 2 == 0:
            w_gate2 = jnp.block([[gla_w_gate2[j, 0], zeros], [zeros, gla_w_gate2[j, 1]]]).astype(BF16)
            q, k, v, r, lf, lb = _gla_in(
                x, mix_norm[i], gla_w_in[j, :, :n_main].astype(BF16), gla_w_in[j, :, n_main:].astype(BF16),
                w_gate2, gla_b_gate[j].reshape(1, 2 * GLA_DK_TOT))
            o = _gla_core(q, k, v, lf, lb)
            mixer, mixer_args = "gla", (o, r, gla_head_norm[j], gla_w_out[j].astype(BF16))
        else:
            w = _mla_weights(mla_w_in[j], mla_q_norm[j], mla_kv_norm[j], mla_w_uq[j], mla_w_ukv[j])
            qf, kf, vt = _mla_in(x, mix_norm[i], w, cos, sin)
            mixer, mixer_args = "mla", (_mla_attn(qf, kf, vt), mla_w_out[j].astype(BF16))
        x = _ffn(x, ffn_norm[i, 1], wgu_all, wd_all, i, 1, mixer=mixer, mixer_args=mixer_args,
                 g_final=final_norm if i == DEPTH - 1 else None)
    return x.reshape(BATCH, SEQ, D_MODEL)
```

```python
import functools
import math

import numpy as np
import jax
import jax.numpy as jnp
from jax import lax
from jax.experimental import pallas as pl
from jax.experimental.pallas import tpu as pltpu

F32 = jnp.float32
BF16 = jnp.bfloat16

D_MODEL = 1024
BATCH = 8
SEQ = 2048
DEPTH = 4
EPS = 1e-6
D_FF = 2816

GLA_HEADS = 4
GLA_DK_TOT = 512
GLA_DV_TOT = 1024
GLA_DK = 128
GLA_DV = 256
GLA_GATE_RANK = 16
GLA_TAU = 16.0

MLA_HEADS = 8
MLA_NOPE = 128
MLA_ROPE = 64
MLA_V = 128
MLA_Q_RANK = 768
MLA_KV_RANK = 256
MLA_QK = MLA_NOPE + MLA_ROPE
ROPE_THETA = 10000.0

LANES = 128
SUBLANES = 8
VMEM_CAPACITY_BYTES = 64 * 1024 * 1024
VMEM_LIMIT_BYTES = VMEM_CAPACITY_BYTES - 6 * 1024 * 1024

TOKEN_TILE = 512
FFN_CHUNK = 256
GLA_CHUNK = 64
GLA_LEVELS = int(math.log2(GLA_CHUNK))
GLA_CUM_ROWS = 256
GLA_SUPER = 1024
ATTN_Q_TILE = 2048
ATTN_KEY_BLOCK = 512
MLA_HEAD_SLOT = 2 * LANES


def _params(n_axes):
    return pltpu.CompilerParams(
        dimension_semantics=("parallel",) * n_axes,
        vmem_limit_bytes=VMEM_LIMIT_BYTES)


def _dot(a, b):
    return jnp.dot(a, b, preferred_element_type=F32)


def _dot_nt(a, b):
    return lax.dot_general(a, b, (((1,), (1,)), ((), ())), preferred_element_type=F32)


def _dot_tn(a, b):
    return lax.dot_general(a, b, (((0,), (0,)), ((), ())), preferred_element_type=F32)


def _rms(x, g):
    return x * lax.rsqrt(jnp.mean(x * x, axis=-1, keepdims=True) + EPS) * g


def _row_spec(tile, width):
    return pl.BlockSpec((tile, width), lambda i: (i, 0))


def _const_spec(shape):
    return pl.BlockSpec(shape, lambda *_: (0,) * len(shape))


def _head_major_spec(heads, width):
    return pl.BlockSpec((heads, TOKEN_TILE, width), lambda i: (0, i, 0))


def _gla_gated(o_ref, r_ref, g_ref):
    g = g_ref[...]
    parts = []
    for hd in range(GLA_HEADS):
        r = r_ref[:, hd * GLA_DV:(hd + 1) * GLA_DV]
        parts.append((_rms(o_ref[hd], g) * (r * jax.nn.sigmoid(r))).astype(BF16))
    return jnp.concatenate(parts, axis=-1)


def _ffn_kernel(x_ref, g_ref, wgu_ref, wd_ref, *rest, mixer, final):
    o_ref = rest[-1]
    x = x_ref[...]
    if mixer == "mla":
        a_ref, wo_ref = rest[:2]
        heads = jnp.concatenate([a_ref[hd] for hd in range(MLA_HEADS)], axis=-1)
        x = x + _dot(heads, wo_ref[...])
    elif mixer == "gla":
        go_ref, gr_ref, gg_ref, wo_ref = rest[:4]
        x = x + _dot(_gla_gated(go_ref, gr_ref, gg_ref), wo_ref[...])
    h = _rms(x, g_ref[...]).astype(BF16)
    acc = None
    for c in range(D_FF // FFN_CHUNK):
        lo = c * FFN_CHUNK
        gate = _dot(h, wgu_ref[:, lo:lo + FFN_CHUNK].astype(BF16))
        up = _dot(h, wgu_ref[:, D_FF + lo:D_FF + lo + FFN_CHUNK].astype(BF16))
        act = (gate * jax.nn.sigmoid(gate) * up).astype(BF16)
        part = _dot(act, wd_ref[lo:lo + FFN_CHUNK, :].astype(BF16))
        acc = part if acc is None else acc + part
    y = x + 0.5 * acc
    if final:
        y = _rms(y, rest[-2][...])
    o_ref[...] = y


def _resident_spec(shape, index):
    return pl.BlockSpec(shape, lambda *_: index, pipeline_mode=pl.Buffered(1))


def _ffn(x, g, wgu_all, wd_all, layer, half, mixer=None, mixer_args=(), g_final=None):
    t = x.shape[0]
    final = g_final is not None
    pick = (layer, half, 0, 0)
    in_specs = [_row_spec(TOKEN_TILE, D_MODEL), _const_spec((1, D_MODEL)),
                _resident_spec((None, None, D_MODEL, 2 * D_FF), pick),
                _resident_spec((None, None, D_FF, D_MODEL), pick)]
    args = [x, g.reshape(1, D_MODEL), wgu_all, wd_all]
    if mixer == "mla":
        a, w_out = mixer_args
        in_specs += [_head_major_spec(MLA_HEADS, MLA_V), _resident_spec(w_out.shape, (0, 0))]
        args += [a, w_out]
    elif mixer == "gla":
        o, r, g_head, w_out = mixer_args
        in_specs += [_head_major_spec(GLA_HEADS, GLA_DV), _row_spec(TOKEN_TILE, GLA_DV_TOT),
                     _const_spec((1, GLA_DV)), _resident_spec(w_out.shape, (0, 0))]
        args += [o, r, g_head.reshape(1, GLA_DV), w_out]
    if final:
        in_specs.append(_const_spec((1, D_MODEL)))
        args.append(g_final.reshape(1, D_MODEL))
    return pl.pallas_call(
        functools.partial(_ffn_kernel, mixer=mixer, final=final),
        grid=(t // TOKEN_TILE,),
        in_specs=in_specs,
        out_specs=_row_spec(TOKEN_TILE, D_MODEL),
        out_shape=jax.ShapeDtypeStruct((t, D_MODEL), F32),
        compiler_params=_params(1),
        name="ffn" + ("_" + mixer if mixer else "") + ("_final" if final else ""),
    )(*args)


def _gla_in_kernel(x_ref, g_ref, w_ref, w2_ref, b2_ref,
                   q_ref, k_ref, v_ref, r_ref, lf_ref, lb_ref):
    n_main = 2 * GLA_DK_TOT + 2 * GLA_DV_TOT
    h = _rms(x_ref[...], g_ref[...]).astype(BF16)
    low = _dot(h, w_ref[:, n_main:].astype(BF16)).astype(BF16)
    z = _dot(low, w2_ref[...]) + b2_ref[...]
    log_a = (jnp.minimum(z, 0.0) - jnp.log(1.0 + jnp.exp(-jnp.abs(z)))) * (math.log2(math.e) / GLA_TAU)
    for hd in range(GLA_HEADS):
        lf_ref[hd] = log_a[:, hd * GLA_DK:(hd + 1) * GLA_DK]
        lb_ref[hd] = log_a[:, GLA_DK_TOT + hd * GLA_DK:GLA_DK_TOT + (hd + 1) * GLA_DK]
    qk = _dot(h, w_ref[:, :2 * GLA_DK_TOT].astype(BF16))
    for hd in range(GLA_HEADS):
        q_ref[hd] = qk[:, hd * GLA_DK:(hd + 1) * GLA_DK] * (GLA_DK ** -0.5)
        k_ref[hd] = qk[:, GLA_DK_TOT + hd * GLA_DK:GLA_DK_TOT + (hd + 1) * GLA_DK]
    v = _dot(h, w_ref[:, 2 * GLA_DK_TOT:2 * GLA_DK_TOT + GLA_DV_TOT].astype(BF16))
    for hd in range(GLA_HEADS):
        v_ref[hd] = v[:, hd * GLA_DV:(hd + 1) * GLA_DV].astype(BF16)
    r_ref[...] = _dot(h, w_ref[:, 2 * GLA_DK_TOT + GLA_DV_TOT:n_main].astype(BF16))


def _gla_in(x, g, w_in_all, layer, w_gate2, b_gate):
    t = x.shape[0]
    per_head = lambda width, dtype: jax.ShapeDtypeStruct((GLA_HEADS, t, width), dtype)
    out_shape = (per_head(GLA_DK, F32), per_head(GLA_DK, F32), per_head(GLA_DV, BF16),
                 jax.ShapeDtypeStruct((t, GLA_DV_TOT), F32), per_head(GLA_DK, F32), per_head(GLA_DK, F32))
    out_spec = lambda s: (_row_spec(TOKEN_TILE, s.shape[1]) if len(s.shape) == 2 else
                          _head_major_spec(GLA_HEADS, s.shape[2]))
    return pl.pallas_call(
        _gla_in_kernel,
        grid=(t // TOKEN_TILE,),
        in_specs=[_row_spec(TOKEN_TILE, D_MODEL), _const_spec((1, D_MODEL)),
                  _resident_spec((None,) + w_in_all.shape[1:], (layer, 0, 0)),
                  _const_spec((2 * GLA_GATE_RANK, 2 * GLA_DK_TOT)), _const_spec((1, 2 * GLA_DK_TOT))],
        out_specs=tuple(out_spec(s) for s in out_shape),
        out_shape=out_shape,
        compiler_params=_params(1),
        name="gla_in",
    )(x, g.reshape(1, D_MODEL), w_in_all, w_gate2, b_gate)


def _gla_constants():
    ch, nl = GLA_CHUNK, GLA_LEVELS
    tri = np.tril(np.ones((ch, ch), np.float32))
    eye = np.eye(GLA_CUM_ROWS // ch, dtype=np.float32)
    cum_fw = np.kron(eye, tri)
    cum_bw = np.kron(eye, tri.T)
    mask = np.zeros((2 * nl + 1, ch, ch), np.float32)
    t = np.arange(ch)[:, None]
    s = np.arange(ch)[None, :]
    for lvl in range(nl):
        w = 1 << lvl
        sel = (t // (2 * w) == s // (2 * w)) & (t % (2 * w) >= w) & (s % (2 * w) < w)
        mask[lvl] = sel
        mask[nl + 1 + lvl] = sel.T
    mask[nl] = np.eye(ch, dtype=np.float32)
    rows = np.arange(ch)
    sign = np.stack([np.where(rows % (2 << lvl) >= (1 << lvl), 1.0, -1.0) for lvl in range(nl)])
    sign[0, 1::2] = 0.0
    sign = np.broadcast_to(sign[:, :, None], (nl, ch, LANES)).astype(np.float32)
    sign = np.concatenate([sign, -sign], axis=0)
    return (jnp.asarray(cum_fw, BF16), jnp.asarray(cum_bw, BF16), jnp.asarray(mask, F32),
            jnp.asarray(sign, F32))


def _level_reference(b, lvl):
    ch = GLA_CHUNK
    w = 1 << lvl
    if lvl == 0:
        return pltpu.roll(b, ch - 1, axis=0)
    if 2 * w >= 2 * SUBLANES:
        parts = [jnp.broadcast_to(b[blk + w:blk + w + 1, :], (2 * w, LANES))
                 for blk in range(0, ch, 2 * w)]
        return parts[0] if len(parts) == 1 else jnp.concatenate(parts, axis=0)
    b3 = b.reshape(ch // SUBLANES, SUBLANES, LANES)
    sub = lax.broadcasted_iota(jnp.int32, b3.shape, 1)
    ref = None
    for blk in range(SUBLANES - 2 * w, -1, -2 * w):
        row = jnp.broadcast_to(b3[:, blk + w:blk + w + 1, :], b3.shape)
        ref = row if ref is None else jnp.where(sub < blk + 2 * w, row, ref)
    return ref.reshape(ch, LANES)


def _gla_core_kernel(q_ref, k_ref, v_ref, lf_ref, lb_ref, cf_ref, cb_ref, mask_ref, sign_ref,
                     o_ref, sf_ref, sb_ref):
    ch, nl, sup = GLA_CHUNK, GLA_LEVELS, GLA_SUPER
    n_super = SEQ // sup
    sf_ref[...] = jnp.zeros_like(sf_ref)
    sb_ref[...] = jnp.zeros_like(sb_ref)
    o_ref[...] = jnp.zeros_like(o_ref)

    def cumulative(row0, la_ref, cum_ref):
        pieces = []
        for r in range(0, sup, GLA_CUM_ROWS):
            la = la_ref[pl.ds(row0 + r, GLA_CUM_ROWS), :]
            la_hi = la.astype(BF16)
            la_lo = (la - la_hi.astype(F32)).astype(BF16)
            pieces.append(_dot(cum_ref[...], la_hi) + _dot(cum_ref[...], la_lo))
        return jnp.concatenate(pieces, axis=0)

    def intra(row0, j, cum, forward):
        rows = pl.ds(row0 + j * ch, ch)
        q = q_ref[rows, :]
        k = k_ref[rows, :]
        v = v_ref[rows, :]
        b = cum[j * ch:(j + 1) * ch]
        edge = b[ch - 1:ch, :] if forward else b[0:1, :]
        p = None
        for lvl in range(nl):
            sign = sign_ref[lvl if forward else nl + lvl]
            x = jnp.exp2((b - _level_reference(b, lvl)) * sign)
            sc = _dot_nt((q * x).astype(BF16), (k * x).astype(BF16))
            sc = sc * mask_ref[lvl if forward else nl + 1 + lvl]
            p = sc if p is None else p + sc
        if forward:
            p = p + _dot_nt(q.astype(BF16), k.astype(BF16)) * mask_ref[nl]
        qd = (q * jnp.exp2(b)).astype(BF16)
        kd = (k * jnp.exp2(edge - b)).astype(BF16)
        decay = jnp.broadcast_to(jnp.exp2(edge), (GLA_DK, GLA_DK)).T
        return rows, jnp.concatenate([qd, p.astype(BF16)], axis=1), v, decay, _dot_tn(kd, v)

    def inter(st_ref, rows, qd_p, v, decay, update):
        st = st_ref[...]
        o_ref[rows, :] += _dot(qd_p, jnp.concatenate([st.astype(BF16), v], axis=0))
        st_ref[...] = st * jnp.concatenate([decay, decay], axis=1) + update

    def body(i, carry):
        n = sup // ch
        row_f = pl.multiple_of(i * sup, sup)
        row_b = pl.multiple_of((n_super - 1 - i) * sup, sup)
        cum_f = cumulative(row_f, lf_ref, cf_ref)
        cum_b = cumulative(row_b, lb_ref, cb_ref)
        done = []
        for j in range(n):
            done.append((sf_ref, intra(row_f, j, cum_f, True)))
            done.append((sb_ref, intra(row_b, n - 1 - j, cum_b, False)))
        for st_ref, parts in done:
            inter(st_ref, *parts)
        return carry

    lax.fori_loop(0, n_super, body, 0)


def _gla_core(q, k, v, lf, lb):
    t = q.shape[1]
    cum_fw, cum_bw, mask, sign = _gla_constants()
    seq_blk = lambda width: pl.BlockSpec((None, SEQ, width), lambda b, h: (h, b, 0))
    return pl.pallas_call(
        _gla_core_kernel,
        grid=(BATCH, GLA_HEADS),
        in_specs=[seq_blk(GLA_DK), seq_blk(GLA_DK), seq_blk(GLA_DV), seq_blk(GLA_DK), seq_blk(GLA_DK),
                  _const_spec(cum_fw.shape), _const_spec(cum_bw.shape), _const_spec(mask.shape),
                  _const_spec(sign.shape)],
        out_specs=seq_blk(GLA_DV),
        out_shape=jax.ShapeDtypeStruct((GLA_HEADS, t, GLA_DV), F32),
        scratch_shapes=[pltpu.VMEM((GLA_DK, GLA_DV), F32), pltpu.VMEM((GLA_DK, GLA_DV), F32)],
        compiler_params=_params(2),
        name="gla_core",
    )(q, k, v, lf, lb, cum_fw, cum_bw, mask, sign)


def _rope_table_kernel(pos_ref, freq_ref, cos_ref, sin_ref):
    ang = pos_ref[...].astype(F32) * freq_ref[...]
    cos_ref[...] = jnp.cos(ang)
    sin_ref[...] = jnp.sin(ang)


def _rope_tables(positions):
    half = MLA_ROPE // 2
    per_row = LANES // half
    t = positions.size
    inv_freq = 1.0 / (ROPE_THETA ** (jnp.arange(0, MLA_ROPE, 2, dtype=F32) / MLA_ROPE))
    pos = jnp.repeat(positions.reshape(t // per_row, per_row), half, axis=1)
    rows = t // per_row
    tile = 512
    cos, sin = pl.pallas_call(
        _rope_table_kernel,
        grid=(rows // tile,),
        in_specs=[_row_spec(tile, LANES), _const_spec((1, LANES))],
        out_specs=(_row_spec(tile, LANES), _row_spec(tile, LANES)),
        out_shape=(jax.ShapeDtypeStruct((rows, LANES), F32),) * 2,
        compiler_params=_params(1),
        name="rope_table",
    )(pos, jnp.tile(inv_freq, per_row).reshape(1, LANES))
    widen = lambda a: jnp.tile(a.reshape(t, half), (1, per_row))
    return widen(cos), widen(sin)


def _mla_in_kernel(x_ref, g_ref, wcq_ref, wckv_ref, wkpe_ref, gq_ref, gkv_ref,
                   wqn_ref, wqp_ref, wqpr_ref, wkn_ref, wv_ref, cos_ref, sin_ref,
                   q_out, k_out, vt_out):
    h = _rms(x_ref[...], g_ref[...]).astype(BF16)
    cos = cos_ref[...]
    sin = sin_ref[...]
    kpe2 = _dot(h, wkpe_ref[...])
    kpe = (kpe2[:, :LANES] * cos + kpe2[:, LANES:] * sin).astype(BF16)
    hq = _rms(_dot(h, wcq_ref[...]), gq_ref[...]).astype(BF16)
    hkv = _rms(_dot(h, wckv_ref[...]), gkv_ref[...]).astype(BF16)
    scale = MLA_QK ** -0.5 * math.log2(math.e)
    qn = _dot(hq, wqn_ref[...])
    pairs = MLA_HEADS // 2
    cos4 = jnp.concatenate([cos] * pairs, axis=-1)
    sin4 = jnp.concatenate([sin] * pairs, axis=-1)
    qp = (_dot(hq, wqp_ref[...]) * cos4 + _dot(hq, wqpr_ref[...]) * sin4) * scale
    kn = _dot(hkv, wkn_ref[...])
    vt_out[...] = _dot(hkv, wv_ref[...]).T.astype(BF16)
    low_half = lax.broadcasted_iota(jnp.int32, (qp.shape[0], LANES), 1) < MLA_ROPE
    for hd in range(MLA_HEADS):
        cols = slice(hd * LANES, (hd + 1) * LANES)
        pair = qp[:, (hd // 2) * LANES:(hd // 2 + 1) * LANES]
        if hd % 2:
            pair = pltpu.roll(pair, MLA_ROPE, axis=1)
        q_out[hd, :, :LANES] = (qn[:, cols] * scale).astype(BF16)
        q_out[hd, :, LANES:] = jnp.where(low_half, pair, 0.0).astype(BF16)
        k_out[hd, :, :LANES] = kn[:, cols].astype(BF16)
        k_out[hd, :, LANES:] = kpe


def _mla_in(x, g, w, cos, sin):
    t = x.shape[0]
    weights = [w["wcq"], w["wckv"], w["wkpe"], w["gq"], w["gkv"],
               w["wqn"], w["wqp"], w["wqpr"], w["wkn"], w["wv"]]
    slots = jax.ShapeDtypeStruct((MLA_HEADS, t, MLA_HEAD_SLOT), BF16)
    out_shape = (slots, slots, jax.ShapeDtypeStruct((MLA_HEADS * MLA_V, t), BF16))
    return pl.pallas_call(
        _mla_in_kernel,
        grid=(t // TOKEN_TILE,),
        in_specs=[_row_spec(TOKEN_TILE, D_MODEL), _const_spec((1, D_MODEL))]
                 + [_const_spec(a.shape) for a in weights]
                 + [_row_spec(TOKEN_TILE, LANES), _row_spec(TOKEN_TILE, LANES)],
        out_specs=(_head_major_spec(MLA_HEADS, MLA_HEAD_SLOT), _head_major_spec(MLA_HEADS, MLA_HEAD_SLOT),
                   pl.BlockSpec((MLA_HEADS * MLA_V, TOKEN_TILE), lambda i: (0, i))),
        out_shape=out_shape,
        compiler_params=_params(1),
        name="mla_in",
    )(x, g.reshape(1, D_MODEL), *weights, cos, sin)


def _mla_attn_kernel(q_ref, k_ref, vt_ref, o_ref, s_ref):
    n_blocks = SEQ // ATTN_KEY_BLOCK

    def scores(j):
        keys = slice(j * ATTN_KEY_BLOCK, (j + 1) * ATTN_KEY_BLOCK)
        s_ref[j % 2] = _dot_nt(k_ref[keys, :], q_ref[...])

    scores(0)
    ones = jnp.ones((2 * SUBLANES, ATTN_KEY_BLOCK), BF16)
    m = acc = None
    for j in range(n_blocks):
        keys = slice(j * ATTN_KEY_BLOCK, (j + 1) * ATTN_KEY_BLOCK)
        if j + 1 < n_blocks:
            scores(j + 1)
        s = s_ref[j % 2]
        m_blk = jnp.max(s, axis=0, keepdims=True)
        vt_ones = jnp.concatenate([vt_ref[:, keys], ones], axis=0)
        if j == 0:
            m = m_blk
            acc = _dot(vt_ones, jnp.exp2(s - m).astype(BF16))
        else:
            m_new = jnp.maximum(m, m_blk)
            acc = jnp.exp2(m - m_new) * acc + _dot(vt_ones, jnp.exp2(s - m_new).astype(BF16))
            m = m_new
    o_ref[...] = (acc[:MLA_V] / acc[MLA_V:MLA_V + 1]).T.astype(o_ref.dtype)


def _mla_attn(q, k, vt):
    t = q.shape[1]
    n_q = SEQ // ATTN_Q_TILE
    return pl.pallas_call(
        _mla_attn_kernel,
        grid=(BATCH, MLA_HEADS, n_q),
        in_specs=[pl.BlockSpec((None, ATTN_Q_TILE, MLA_HEAD_SLOT), lambda b, h, i: (h, b * n_q + i, 0)),
                  pl.BlockSpec((None, SEQ, MLA_HEAD_SLOT), lambda b, h, i: (h, b, 0)),
                  pl.BlockSpec((MLA_V, SEQ), lambda b, h, i: (h, b))],
        out_specs=pl.BlockSpec((None, ATTN_Q_TILE, MLA_V), lambda b, h, i: (h, b * n_q + i, 0)),
        out_shape=jax.ShapeDtypeStruct((MLA_HEADS, t, MLA_V), BF16),
        scratch_shapes=[pltpu.VMEM((2, ATTN_KEY_BLOCK, ATTN_Q_TILE), F32)],
        compiler_params=_params(3),
        name="mla_attn",
    )(q, k, vt)


def _rotate_half_columns(w):
    shape = w.shape
    w = w.reshape(shape[0], -1, 2, MLA_ROPE // 2)
    return jnp.stack([-w[:, :, 1], w[:, :, 0]], axis=2).reshape(shape)


def _pad_rope_columns(w):
    k = w.shape[0]
    w = w.reshape(k, -1, MLA_ROPE)
    return jnp.pad(w, ((0, 0), (0, 0), (0, LANES - MLA_ROPE))).reshape(k, -1)


def _mla_weights(w_in, q_norm, kv_norm, w_uq, w_ukv):
    w_kpe = w_in[:, MLA_Q_RANK + MLA_KV_RANK:]
    w_uq = w_uq.reshape(MLA_Q_RANK, MLA_HEADS, MLA_QK)
    w_uq_rope = w_uq[:, :, MLA_NOPE:].reshape(MLA_Q_RANK, MLA_HEADS * MLA_ROPE)
    w_ukv = w_ukv.reshape(MLA_KV_RANK, MLA_HEADS, MLA_NOPE + MLA_V)
    return {
        "wcq": w_in[:, :MLA_Q_RANK].astype(BF16),
        "wckv": w_in[:, MLA_Q_RANK:MLA_Q_RANK + MLA_KV_RANK].astype(BF16),
        "wkpe": jnp.concatenate([_pad_rope_columns(w_kpe),
                                 _pad_rope_columns(_rotate_half_columns(w_kpe))], axis=1).astype(BF16),
        "gq": q_norm.reshape(1, MLA_Q_RANK),
        "gkv": kv_norm.reshape(1, MLA_KV_RANK),
        "wqn": w_uq[:, :, :MLA_NOPE].reshape(MLA_Q_RANK, MLA_HEADS * MLA_NOPE).astype(BF16),
        "wqp": w_uq_rope.astype(BF16),
        "wqpr": _rotate_half_columns(w_uq_rope).astype(BF16),
        "wkn": w_ukv[:, :, :MLA_NOPE].reshape(MLA_KV_RANK, MLA_HEADS * MLA_NOPE).astype(BF16),
        "wv": w_ukv[:, :, MLA_NOPE:].reshape(MLA_KV_RANK, MLA_HEADS * MLA_V).astype(BF16),
    }


def kernel(x, positions, ffn_norm, ffn_w_gu, ffn_w_down, mix_norm, gla_w_in, gla_w_gate2, gla_b_gate,
           gla_head_norm, gla_w_out, mla_w_in, mla_q_norm, mla_kv_norm, mla_w_uq, mla_w_ukv, mla_w_out,
           final_norm):
    assert x.shape == (BATCH, SEQ, D_MODEL)
    t = BATCH * SEQ
    x = x.reshape(t, D_MODEL)
    cos, sin = _rope_tables(positions)
    zeros = jnp.zeros((GLA_GATE_RANK, GLA_DK_TOT), F32)
    wgu_all = ffn_w_gu
    wd_all = ffn_w_down
    for i in range(DEPTH):
        x = _ffn(x, ffn_norm[i, 0], wgu_all, wd_all, i, 0)
        j = i // 2
        if i % 2 == 0:
            w_gate2 = jnp.block([[gla_w_gate2[j, 0], zeros], [zeros, gla_w_gate2[j, 1]]]).astype(BF16)
            q, k, v, r, lf, lb = _gla_in(
                x, mix_norm[i], gla_w_in, j, w_gate2, gla_b_gate[j].reshape(1, 2 * GLA_DK_TOT))
            o = _gla_core(q, k, v, lf, lb)
            mixer, mixer_args = "gla", (o, r, gla_head_norm[j], gla_w_out[j].astype(BF16))
        else:
            w = _mla_weights(mla_w_in[j], mla_q_norm[j], mla_kv_norm[j], mla_w_uq[j], mla_w_ukv[j])
            qf, kf, vt = _mla_in(x, mix_norm[i], w, cos, sin)
            mixer, mixer_args = "mla", (_mla_attn(qf, kf, vt), mla_w_out[j].astype(BF16))
        x = _ffn(x, ffn_norm[i, 1], wgu_all, wd_all, i, 1, mixer=mixer, mixer_args=mixer_args,
                 g_final=final_norm if i == DEPTH - 1 else None)
    return x.reshape(BATCH, SEQ, D_MODEL)
```

```python
import functools
import math

import numpy as np
import jax
import jax.numpy as jnp
from jax import lax
from jax.experimental import pallas as pl
from jax.experimental.pallas import tpu as pltpu

F32 = jnp.float32
BF16 = jnp.bfloat16

D_MODEL = 1024
BATCH = 8
SEQ = 2048
DEPTH = 4
EPS = 1e-6
D_FF = 2816

GLA_HEADS = 4
GLA_DK_TOT = 512
GLA_DV_TOT = 1024
GLA_DK = 128
GLA_DV = 256
GLA_GATE_RANK = 16
GLA_TAU = 16.0

MLA_HEADS = 8
MLA_NOPE = 128
MLA_ROPE = 64
MLA_V = 128
MLA_Q_RANK = 768
MLA_KV_RANK = 256
MLA_QK = MLA_NOPE + MLA_ROPE
ROPE_THETA = 10000.0

LANES = 128
SUBLANES = 8
VMEM_CAPACITY_BYTES = 64 * 1024 * 1024
VMEM_LIMIT_BYTES = VMEM_CAPACITY_BYTES - 6 * 1024 * 1024

TOKEN_TILE = 512
FFN_CHUNK = 256
GLA_CHUNK = 64
GLA_LEVELS = int(math.log2(GLA_CHUNK))
GLA_CUM_ROWS = 256
GLA_SUPER = 1024
ATTN_Q_TILE = 2048
ATTN_KEY_BLOCK = 512
MLA_HEAD_SLOT = 2 * LANES


def _params(n_axes):
    return pltpu.CompilerParams(
        dimension_semantics=("parallel",) * n_axes,
        vmem_limit_bytes=VMEM_LIMIT_BYTES)


def _dot(a, b):
    return jnp.dot(a, b, preferred_element_type=F32)


def _dot_nt(a, b):
    return lax.dot_general(a, b, (((1,), (1,)), ((), ())), preferred_element_type=F32)


def _dot_tn(a, b):
    return lax.dot_general(a, b, (((0,), (0,)), ((), ())), preferred_element_type=F32)


def _rms(x, g):
    return x * lax.rsqrt(jnp.mean(x * x, axis=-1, keepdims=True) + EPS) * g


def _row_spec(tile, width):
    return pl.BlockSpec((tile, width), lambda i: (i, 0))


def _const_spec(shape):
    return pl.BlockSpec(shape, lambda *_: (0,) * len(shape))


def _head_major_spec(heads, width):
    return pl.BlockSpec((heads, TOKEN_TILE, width), lambda i: (0, i, 0))


def _gla_gated(o_ref, r_ref, g_ref):
    g = g_ref[...]
    parts = []
    for hd in range(GLA_HEADS):
        r = r_ref[:, hd * GLA_DV:(hd + 1) * GLA_DV]
        parts.append((_rms(o_ref[hd], g) * (r * jax.nn.sigmoid(r))).astype(BF16))
    return jnp.concatenate(parts, axis=-1)


def _ffn_kernel(x_ref, g_ref, wgu_ref, wd_ref, *rest, mixer, final):
    o_ref = rest[-1]
    x = x_ref[...]
    if mixer == "mla":
        a_ref, wo_ref = rest[:2]
        heads = jnp.concatenate([a_ref[hd] for hd in range(MLA_HEADS)], axis=-1)
        x = x + _dot(heads, wo_ref[...])
    elif mixer == "gla":
        go_ref, gr_ref, gg_ref, wo_ref = rest[:4]
        x = x + _dot(_gla_gated(go_ref, gr_ref, gg_ref), wo_ref[...])
    h = _rms(x, g_ref[...]).astype(BF16)
    acc = None
    for c in range(D_FF // FFN_CHUNK):
        lo = c * FFN_CHUNK
        gate = _dot(h, wgu_ref[:, lo:lo + FFN_CHUNK].astype(BF16))
        up = _dot(h, wgu_ref[:, D_FF + lo:D_FF + lo + FFN_CHUNK].astype(BF16))
        act = (gate * jax.nn.sigmoid(gate) * up).astype(BF16)
        part = _dot(act, wd_ref[lo:lo + FFN_CHUNK, :].astype(BF16))
        acc = part if acc is None else acc + part
    y = x + 0.5 * acc
    if final:
        y = _rms(y, rest[-2][...])
    o_ref[...] = y


def _resident_spec(shape, index):
    return pl.BlockSpec(shape, lambda *_: index, pipeline_mode=pl.Buffered(1))


def _ffn(x, g, wgu_all, wd_all, layer, half, mixer=None, mixer_args=(), g_final=None):
    t = x.shape[0]
    final = g_final is not None
    pick = (layer, half, 0, 0)
    in_specs = [_row_spec(TOKEN_TILE, D_MODEL), _const_spec((1, D_MODEL)),
                _resident_spec((None, None, D_MODEL, 2 * D_FF), pick),
                _resident_spec((None, None, D_FF, D_MODEL), pick)]
    args = [x, g.reshape(1, D_MODEL), wgu_all, wd_all]
    if mixer == "mla":
        a, w_out = mixer_args
        in_specs += [_head_major_spec(MLA_HEADS, MLA_V), _resident_spec(w_out.shape, (0, 0))]
        args += [a, w_out]
    elif mixer == "gla":
        o, r, g_head, w_out = mixer_args
        in_specs += [_head_major_spec(GLA_HEADS, GLA_DV), _row_spec(TOKEN_TILE, GLA_DV_TOT),
                     _const_spec((1, GLA_DV)), _resident_spec(w_out.shape, (0, 0))]
        args += [o, r, g_head.reshape(1, GLA_DV), w_out]
    if final:
        in_specs.append(_const_spec((1, D_MODEL)))
        args.append(g_final.reshape(1, D_MODEL))
    return pl.pallas_call(
        functools.partial(_ffn_kernel, mixer=mixer, final=final),
        grid=(t // TOKEN_TILE,),
        in_specs=in_specs,
        out_specs=_row_spec(TOKEN_TILE, D_MODEL),
        out_shape=jax.ShapeDtypeStruct((t, D_MODEL), F32),
        compiler_params=_params(1),
        name="ffn" + ("_" + mixer if mixer else "") + ("_final" if final else ""),
    )(*args)


def _gla_in_kernel(x_ref, g_ref, w_ref, w2_ref, b2_ref,
                   q_ref, k_ref, v_ref, r_ref, lf_ref, lb_ref):
    n_main = 2 * GLA_DK_TOT + 2 * GLA_DV_TOT
    h = _rms(x_ref[...], g_ref[...]).astype(BF16)
    low = _dot(h, w_ref[:, n_main:].astype(BF16)).astype(BF16)
    z = _dot(low, w2_ref[...]) + b2_ref[...]
    log_a = (jnp.minimum(z, 0.0) - jnp.log(1.0 + jnp.exp(-jnp.abs(z)))) * (math.log2(math.e) / GLA_TAU)
    for hd in range(GLA_HEADS):
        lf_ref[hd] = log_a[:, hd * GLA_DK:(hd + 1) * GLA_DK]
        lb_ref[hd] = log_a[:, GLA_DK_TOT + hd * GLA_DK:GLA_DK_TOT + (hd + 1) * GLA_DK]
    qk = _dot(h, w_ref[:, :2 * GLA_DK_TOT].astype(BF16))
    for hd in range(GLA_HEADS):
        q_ref[hd] = qk[:, hd * GLA_DK:(hd + 1) * GLA_DK] * (GLA_DK ** -0.5)
        k_ref[hd] = qk[:, GLA_DK_TOT + hd * GLA_DK:GLA_DK_TOT + (hd + 1) * GLA_DK]
    v = _dot(h, w_ref[:, 2 * GLA_DK_TOT:2 * GLA_DK_TOT + GLA_DV_TOT].astype(BF16))
    for hd in range(GLA_HEADS):
        v_ref[hd] = v[:, hd * GLA_DV:(hd + 1) * GLA_DV].astype(BF16)
    r_ref[...] = _dot(h, w_ref[:, 2 * GLA_DK_TOT + GLA_DV_TOT:n_main].astype(BF16))


def _gla_in(x, g, w_in_all, layer, w_gate2, b_gate):
    t = x.shape[0]
    per_head = lambda width, dtype: jax.ShapeDtypeStruct((GLA_HEADS, t, width), dtype)
    out_shape = (per_head(GLA_DK, F32), per_head(GLA_DK, F32), per_head(GLA_DV, BF16),
                 jax.ShapeDtypeStruct((t, GLA_DV_TOT), F32), per_head(GLA_DK, F32), per_head(GLA_DK, F32))
    out_spec = lambda s: (_row_spec(TOKEN_TILE, s.shape[1]) if len(s.shape) == 2 else
                          _head_major_spec(GLA_HEADS, s.shape[2]))
    return pl.pallas_call(
        _gla_in_kernel,
        grid=(t // TOKEN_TILE,),
        in_specs=[_row_spec(TOKEN_TILE, D_MODEL), _const_spec((1, D_MODEL)),
                  _resident_spec((None,) + w_in_all.shape[1:], (layer, 0, 0)),
                  _const_spec((2 * GLA_GATE_RANK, 2 * GLA_DK_TOT)), _const_spec((1, 2 * GLA_DK_TOT))],
        out_specs=tuple(out_spec(s) for s in out_shape),
        out_shape=out_shape,
        compiler_params=_params(1),
        name="gla_in",
    )(x, g.reshape(1, D_MODEL), w_in_all, w_gate2, b_gate)


def _gla_constants():
    ch, nl = GLA_CHUNK, GLA_LEVELS
    tri = np.tril(np.ones((ch, ch), np.float32))
    eye = np.eye(GLA_CUM_ROWS // ch, dtype=np.float32)
    cum_fw = np.kron(eye, tri)
    cum_bw = np.kron(eye, tri.T)
    mask = np.zeros((2 * nl + 1, ch, ch), np.float32)
    t = np.arange(ch)[:, None]
    s = np.arange(ch)[None, :]
    for lvl in range(nl):
        w = 1 << lvl
        sel = (t // (2 * w) == s // (2 * w)) & (t % (2 * w) >= w) & (s % (2 * w) < w)
        mask[lvl] = sel
        mask[nl + 1 + lvl] = sel.T
    mask[nl] = np.eye(ch, dtype=np.float32)
    rows = np.arange(ch)
    sign = np.stack([np.where(rows % (2 << lvl) >= (1 << lvl), 1.0, -1.0) for lvl in range(nl)])
    sign[0, 1::2] = 0.0
    sign = np.broadcast_to(sign[:, :, None], (nl, ch, LANES)).astype(np.float32)
    sign = np.concatenate([sign, -sign], axis=0)
    return (jnp.asarray(cum_fw, BF16), jnp.asarray(cum_bw, BF16), jnp.asarray(mask, F32),
            jnp.asarray(sign, F32))


def _level_reference(b, lvl):
    ch = GLA_CHUNK
    w = 1 << lvl
    if lvl == 0:
        return pltpu.roll(b, ch - 1, axis=0)
    if 2 * w >= 2 * SUBLANES:
        parts = [jnp.broadcast_to(b[blk + w:blk + w + 1, :], (2 * w, LANES))
                 for blk in range(0, ch, 2 * w)]
        return parts[0] if len(parts) == 1 else jnp.concatenate(parts, axis=0)
    b3 = b.reshape(ch // SUBLANES, SUBLANES, LANES)
    sub = lax.broadcasted_iota(jnp.int32, b3.shape, 1)
    ref = None
    for blk in range(SUBLANES - 2 * w, -1, -2 * w):
        row = jnp.broadcast_to(b3[:, blk + w:blk + w + 1, :], b3.shape)
        ref = row if ref is None else jnp.where(sub < blk + 2 * w, row, ref)
    return ref.reshape(ch, LANES)


def _gla_core_kernel(q_ref, k_ref, v_ref, lf_ref, lb_ref, cf_ref, cb_ref, mask_ref, sign_ref,
                     o_ref, sf_ref, sb_ref):
    ch, nl, sup = GLA_CHUNK, GLA_LEVELS, GLA_SUPER
    n_super = SEQ // sup
    sf_ref[...] = jnp.zeros_like(sf_ref)
    sb_ref[...] = jnp.zeros_like(sb_ref)
    o_ref[...] = jnp.zeros_like(o_ref)

    def cumulative(row0, la_ref, cum_ref):
        pieces = []
        for r in range(0, sup, GLA_CUM_ROWS):
            la = la_ref[pl.ds(row0 + r, GLA_CUM_ROWS), :]
            la_hi = la.astype(BF16)
            la_lo = (la - la_hi.astype(F32)).astype(BF16)
            pieces.append(_dot(cum_ref[...], la_hi) + _dot(cum_ref[...], la_lo))
        return jnp.concatenate(pieces, axis=0)

    def intra(row0, j, cum, forward):
        rows = pl.ds(row0 + j * ch, ch)
        q = q_ref[rows, :]
        k = k_ref[rows, :]
        v = v_ref[rows, :]
        b = cum[j * ch:(j + 1) * ch]
        edge = b[ch - 1:ch, :] if forward else b[0:1, :]
        p = None
        for lvl in range(nl):
            sign = sign_ref[lvl if forward else nl + lvl]
            x = jnp.exp2((b - _level_reference(b, lvl)) * sign)
            sc = _dot_nt((q * x).astype(BF16), (k * x).astype(BF16))
            sc = sc * mask_ref[lvl if forward else nl + 1 + lvl]
            p = sc if p is None else p + sc
        if forward:
            p = p + _dot_nt(q.astype(BF16), k.astype(BF16)) * mask_ref[nl]
        qd = (q * jnp.exp2(b)).astype(BF16)
        kd = (k * jnp.exp2(edge - b)).astype(BF16)
        decay = jnp.broadcast_to(jnp.exp2(edge), (GLA_DK, GLA_DK)).T
        return rows, jnp.concatenate([qd, p.astype(BF16)], axis=1), v, decay, _dot_tn(kd, v)

    def inter(st_ref, rows, qd_p, v, decay, update):
        st = st_ref[...]
        o_ref[rows, :] += _dot(qd_p, jnp.concatenate([st.astype(BF16), v], axis=0))
        st_ref[...] = st * jnp.concatenate([decay, decay], axis=1) + update

    def body(i, carry):
        n = sup // ch
        row_f = pl.multiple_of(i * sup, sup)
        row_b = pl.multiple_of((n_super - 1 - i) * sup, sup)
        cum_f = cumulative(row_f, lf_ref, cf_ref)
        cum_b = cumulative(row_b, lb_ref, cb_ref)
        done = []
        for j in range(n):
            done.append((sf_ref, intra(row_f, j, cum_f, True)))
            done.append((sb_ref, intra(row_b, n - 1 - j, cum_b, False)))
        for st_ref, parts in done:
            inter(st_ref, *parts)
        return carry

    lax.fori_loop(0, n_super, body, 0)


def _gla_core(q, k, v, lf, lb):
    t = q.shape[1]
    cum_fw, cum_bw, mask, sign = _gla_constants()
    seq_blk = lambda width: pl.BlockSpec((None, SEQ, width), lambda b, h: (h, b, 0))
    return pl.pallas_call(
        _gla_core_kernel,
        grid=(BATCH, GLA_HEADS),
        in_specs=[seq_blk(GLA_DK), seq_blk(GLA_DK), seq_blk(GLA_DV), seq_blk(GLA_DK), seq_blk(GLA_DK),
                  _const_spec(cum_fw.shape), _const_spec(cum_bw.shape), _const_spec(mask.shape),
                  _const_spec(sign.shape)],
        out_specs=seq_blk(GLA_DV),
        out_shape=jax.ShapeDtypeStruct((GLA_HEADS, t, GLA_DV), F32),
        scratch_shapes=[pltpu.VMEM((GLA_DK, GLA_DV), F32), pltpu.VMEM((GLA_DK, GLA_DV), F32)],
        compiler_params=_params(2),
        name="gla_core",
    )(q, k, v, lf, lb, cum_fw, cum_bw, mask, sign)


def _rope_table_kernel(pos_ref, freq_ref, cos_ref, sin_ref):
    ang = pos_ref[...].astype(F32) * freq_ref[...]
    cos_ref[...] = jnp.cos(ang)
    sin_ref[...] = jnp.sin(ang)


def _rope_tables(positions):
    t = positions.size
    inv_freq = 1.0 / (ROPE_THETA ** (jnp.arange(0, MLA_ROPE, 2, dtype=F32) / MLA_ROPE))
    freq = jnp.tile(inv_freq, 2 * LANES // MLA_ROPE).reshape(1, LANES)
    tile = 2048
    return pl.pallas_call(
        _rope_table_kernel,
        grid=(t // tile,),
        in_specs=[_row_spec(tile, 1), _const_spec((1, LANES))],
        out_specs=(_row_spec(tile, LANES), _row_spec(tile, LANES)),
        out_shape=(jax.ShapeDtypeStruct((t, LANES), F32),) * 2,
        compiler_params=_params(1),
        name="rope_table",
    )(positions.reshape(t, 1), freq)


def _mla_in_kernel(x_ref, g_ref, wcq_ref, wckv_ref, wkpe_ref, gq_ref, gkv_ref,
                   wqn_ref, wqp_ref, wqpr_ref, wkn_ref, wv_ref, cos_ref, sin_ref,
                   q_out, k_out, vt_out):
    h = _rms(x_ref[...], g_ref[...]).astype(BF16)
    cos = cos_ref[...]
    sin = sin_ref[...]
    kpe2 = _dot(h, wkpe_ref[...])
    kpe = (kpe2[:, :LANES] * cos + kpe2[:, LANES:] * sin).astype(BF16)
    hq = _rms(_dot(h, wcq_ref[...]), gq_ref[...]).astype(BF16)
    hkv = _rms(_dot(h, wckv_ref[...]), gkv_ref[...]).astype(BF16)
    scale = MLA_QK ** -0.5 * math.log2(math.e)
    qn = _dot(hq, wqn_ref[...])
    pairs = MLA_HEADS // 2
    cos4 = jnp.concatenate([cos] * pairs, axis=-1)
    sin4 = jnp.concatenate([sin] * pairs, axis=-1)
    qp = (_dot(hq, wqp_ref[...]) * cos4 + _dot(hq, wqpr_ref[...]) * sin4) * scale
    kn = _dot(hkv, wkn_ref[...])
    vt_out[...] = _dot(hkv, wv_ref[...]).T.astype(BF16)
    low_half = lax.broadcasted_iota(jnp.int32, (qp.shape[0], LANES), 1) < MLA_ROPE
    for hd in range(MLA_HEADS):
        cols = slice(hd * LANES, (hd + 1) * LANES)
        pair = qp[:, (hd // 2) * LANES:(hd // 2 + 1) * LANES]
        if hd % 2:
            pair = pltpu.roll(pair, MLA_ROPE, axis=1)
        q_out[hd, :, :LANES] = (qn[:, cols] * scale).astype(BF16)
        q_out[hd, :, LANES:] = jnp.where(low_half, pair, 0.0).astype(BF16)
        k_out[hd, :, :LANES] = kn[:, cols].astype(BF16)
        k_out[hd, :, LANES:] = kpe


def _mla_in(x, g, w, layer, cos, sin):
    t = x.shape[0]
    weights = [w["wcq"], w["wckv"], w["wkpe"], w["gq"], w["gkv"],
               w["wqn"], w["wqp"], w["wqpr"], w["wkn"], w["wv"]]
    pick = lambda a: pl.BlockSpec((None,) + a.shape[1:], lambda i: (layer, 0, 0))
    slots = jax.ShapeDtypeStruct((MLA_HEADS, t, MLA_HEAD_SLOT), BF16)
    out_shape = (slots, slots, jax.ShapeDtypeStruct((MLA_HEADS * MLA_V, t), BF16))
    return pl.pallas_call(
        _mla_in_kernel,
        grid=(t // TOKEN_TILE,),
        in_specs=[_row_spec(TOKEN_TILE, D_MODEL), _const_spec((1, D_MODEL))]
                 + [pick(a) for a in weights]
                 + [_row_spec(TOKEN_TILE, LANES), _row_spec(TOKEN_TILE, LANES)],
        out_specs=(_head_major_spec(MLA_HEADS, MLA_HEAD_SLOT), _head_major_spec(MLA_HEADS, MLA_HEAD_SLOT),
                   pl.BlockSpec((MLA_HEADS * MLA_V, TOKEN_TILE), lambda i: (0, i))),
        out_shape=out_shape,
        compiler_params=_params(1),
        name="mla_in",
    )(x, g.reshape(1, D_MODEL), *weights, cos, sin)


def _mla_attn_kernel(q_ref, k_ref, vt_ref, o_ref, s_ref):
    n_blocks = SEQ // ATTN_KEY_BLOCK

    def scores(j):
        keys = slice(j * ATTN_KEY_BLOCK, (j + 1) * ATTN_KEY_BLOCK)
        s_ref[j % 2] = _dot_nt(k_ref[keys, :], q_ref[...])

    scores(0)
    ones = jnp.ones((2 * SUBLANES, ATTN_KEY_BLOCK), BF16)
    m = acc = None
    for j in range(n_blocks):
        keys = slice(j * ATTN_KEY_BLOCK, (j + 1) * ATTN_KEY_BLOCK)
        if j + 1 < n_blocks:
            scores(j + 1)
        s = s_ref[j % 2]
        m_blk = jnp.max(s, axis=0, keepdims=True)
        vt_ones = jnp.concatenate([vt_ref[:, keys], ones], axis=0)
        if j == 0:
            m = m_blk
            acc = _dot(vt_ones, jnp.exp2(s - m).astype(BF16))
        else:
            m_new = jnp.maximum(m, m_blk)
            acc = jnp.exp2(m - m_new) * acc + _dot(vt_ones, jnp.exp2(s - m_new).astype(BF16))
            m = m_new
    o_ref[...] = (acc[:MLA_V] / acc[MLA_V:MLA_V + 1]).T.astype(o_ref.dtype)


def _mla_attn(q, k, vt):
    t = q.shape[1]
    n_q = SEQ // ATTN_Q_TILE
    return pl.pallas_call(
        _mla_attn_kernel,
        grid=(BATCH, MLA_HEADS, n_q),
        in_specs=[pl.BlockSpec((None, ATTN_Q_TILE, MLA_HEAD_SLOT), lambda b, h, i: (h, b * n_q + i, 0)),
                  pl.BlockSpec((None, SEQ, MLA_HEAD_SLOT), lambda b, h, i: (h, b, 0)),
                  pl.BlockSpec((MLA_V, SEQ), lambda b, h, i: (h, b))],
        out_specs=pl.BlockSpec((None, ATTN_Q_TILE, MLA_V), lambda b, h, i: (h, b * n_q + i, 0)),
        out_shape=jax.ShapeDtypeStruct((MLA_HEADS, t, MLA_V), BF16),
        scratch_shapes=[pltpu.VMEM((2, ATTN_KEY_BLOCK, ATTN_Q_TILE), F32)],
        compiler_params=_params(3),
        name="mla_attn",
    )(q, k, vt)


def _rotate_half_columns(w):
    shape = w.shape
    w = w.reshape(shape[0], -1, 2, MLA_ROPE // 2)
    return jnp.stack([-w[:, :, 1], w[:, :, 0]], axis=2).reshape(shape)


def _pad_rope_columns(w):
    k = w.shape[0]
    w = w.reshape(k, -1, MLA_ROPE)
    return jnp.pad(w, ((0, 0), (0, 0), (0, LANES - MLA_ROPE))).reshape(k, -1)


def _mla_weights(w_in, q_norm, kv_norm, w_uq, w_ukv):
    w_kpe = w_in[:, MLA_Q_RANK + MLA_KV_RANK:]
    w_uq = w_uq.reshape(MLA_Q_RANK, MLA_HEADS, MLA_QK)
    w_uq_rope = w_uq[:, :, MLA_NOPE:].reshape(MLA_Q_RANK, MLA_HEADS * MLA_ROPE)
    w_ukv = w_ukv.reshape(MLA_KV_RANK, MLA_HEADS, MLA_NOPE + MLA_V)
    return {
        "wcq": w_in[:, :MLA_Q_RANK].astype(BF16),
        "wckv": w_in[:, MLA_Q_RANK:MLA_Q_RANK + MLA_KV_RANK].astype(BF16),
        "wkpe": jnp.concatenate([_pad_rope_columns(w_kpe),
                                 _pad_rope_columns(_rotate_half_columns(w_kpe))], axis=1).astype(BF16),
        "gq": q_norm.reshape(1, MLA_Q_RANK),
        "gkv": kv_norm.reshape(1, MLA_KV_RANK),
        "wqn": w_uq[:, :, :MLA_NOPE].reshape(MLA_Q_RANK, MLA_HEADS * MLA_NOPE).astype(BF16),
        "wqp": w_uq_rope.astype(BF16),
        "wqpr": _rotate_half_columns(w_uq_rope).astype(BF16),
        "wkn": w_ukv[:, :, :MLA_NOPE].reshape(MLA_KV_RANK, MLA_HEADS * MLA_NOPE).astype(BF16),
        "wv": w_ukv[:, :, MLA_NOPE:].reshape(MLA_KV_RANK, MLA_HEADS * MLA_V).astype(BF16),
    }


def kernel(x, positions, ffn_norm, ffn_w_gu, ffn_w_down, mix_norm, gla_w_in, gla_w_gate2, gla_b_gate,
           gla_head_norm, gla_w_out, mla_w_in, mla_q_norm, mla_kv_norm, mla_w_uq, mla_w_ukv, mla_w_out,
           final_norm):
    assert x.shape == (BATCH, SEQ, D_MODEL)
    t = BATCH * SEQ
    x = x.reshape(t, D_MODEL)
    cos, sin = _rope_tables(positions)
    mla_w = jax.vmap(_mla_weights)(mla_w_in, mla_q_norm, mla_kv_norm, mla_w_uq, mla_w_ukv)
    zeros = jnp.zeros((GLA_GATE_RANK, GLA_DK_TOT), F32)
    wgu_all = ffn_w_gu
    wd_all = ffn_w_down
    for i in range(DEPTH):
        x = _ffn(x, ffn_norm[i, 0], wgu_all, wd_all, i, 0)
        j = i // 2
        if i % 2 == 0:
            w_gate2 = jnp.block([[gla_w_gate2[j, 0], zeros], [zeros, gla_w_gate2[j, 1]]]).astype(BF16)
            q, k, v, r, lf, lb = _gla_in(
                x, mix_norm[i], gla_w_in, j, w_gate2, gla_b_gate[j].reshape(1, 2 * GLA_DK_TOT))
            o = _gla_core(q, k, v, lf, lb)
            mixer, mixer_args = "gla", (o, r, gla_head_norm[j], gla_w_out[j].astype(BF16))
        else:
            qf, kf, vt = _mla_in(x, mix_norm[i], mla_w, j, cos, sin)
            mixer, mixer_args = "mla", (_mla_attn(qf, kf, vt), mla_w_out[j].astype(BF16))
        x = _ffn(x, ffn_norm[i, 1], wgu_all, wd_all, i, 1, mixer=mixer, mixer_args=mixer_args,
                 g_final=final_norm if i == DEPTH - 1 else None)
    return x.reshape(BATCH, SEQ, D_MODEL)
```
